```python
import jax, jax.numpy as jnp
from jax import lax
import numpy as np

D_MODEL = 1024
BATCH = 8
SEQ = 4096
DEPTH = 1

NSA_HEADS = 8
NSA_KV_GROUPS = 2
NSA_HEAD_DIM = 64
NSA_WIDTH = NSA_HEADS * NSA_HEAD_DIM
NSA_BRANCHES = 3
CMP_STRIDE = 16
CMP_LEN = 2 * CMP_STRIDE
CMP_HIDDEN = 256
SEL_BLOCK = 64
SEL_TOPN = 16
SEL_QCHUNK = 64
WINDOW = 512
WIN_QBLOCK = 128

HGRN_WIDTH = D_MODEL - NSA_WIDTH
HGRN_EXPAND = 128
HGRN_HEADS = HGRN_WIDTH // HGRN_EXPAND
HGRN_DK = HGRN_EXPAND
HGRN_DV = HGRN_WIDTH // HGRN_HEADS
HGRN_CHUNK = 64

MIX_WIDTH = NSA_WIDTH + HGRN_WIDTH

MOE_GROUPS = 4
EXPERTS_PER_GROUP = 8
N_EXPERTS = MOE_GROUPS * EXPERTS_PER_GROUP
MOE_TOPK = 2
EXPERT_FF = 512
MOE_BLOCK = 256

RMS_EPS = 1e-6
NEG_INF = -1e30

KV_DIM = NSA_KV_GROUPS * NSA_HEAD_DIM
IN_SPLITS = (NSA_WIDTH,) + (KV_DIM,) * (2 * NSA_BRANCHES) + (NSA_HEADS * NSA_BRANCHES,) + (HGRN_WIDTH,) * 4
IN_COLS = sum(IN_SPLITS)

kernel_name = 'hybrid_nsa_hgrn2_hmoe'


def rmsnorm(x, g):
    xf = x.astype(jnp.float32)
    y = xf * lax.rsqrt(jnp.mean(xf * xf, axis=-1, keepdims=True) + RMS_EPS)
    return (y * g.astype(jnp.float32)).astype(x.dtype)


def nsa_mixer(q, kc_raw, vc_raw, ks_raw, vs_raw, kw_raw, vw_raw, gate_logits,
              g_q, g_k, pos_k, pos_v, wk1, wk2, wv1, wv2):
    B, S = q.shape[0], q.shape[1]
    G, R, DH = NSA_KV_GROUPS, NSA_HEADS // NSA_KV_GROUPS, NSA_HEAD_DIM
    dt = q.dtype
    scale = DH ** -0.5
    qn = rmsnorm(q, g_q).reshape(B, S, G, R, DH).transpose(0, 2, 3, 1, 4)
    t_pos = jnp.arange(S)

    def to_groups(t):
        return t.reshape(B, S, G, DH).transpose(0, 2, 1, 3)

    nc = S // CMP_STRIDE - 1

    def compress(t, pos, w1, w2):
        ch = to_groups(t).reshape(B, G, S // CMP_STRIDE, CMP_STRIDE, DH)
        blk = jnp.concatenate([ch[:, :, :-1], ch[:, :, 1:]], axis=3) + pos
        hid = jax.nn.silu(blk.reshape(B, G, nc, CMP_LEN * DH) @ w1)
        return hid @ w2

    kc = rmsnorm(compress(kc_raw, pos_k, wk1, wk2), g_k[0])
    vc = compress(vc_raw, pos_v, wv1, wv2)
    s_c = jnp.einsum('bgrsd,bgcd->bgrsc', qn, kc).astype(jnp.float32) * scale
    mask_c = (jnp.arange(nc) * CMP_STRIDE + CMP_LEN - 1)[None, :] <= t_pos[:, None]
    p_c = jax.nn.softmax(jnp.where(mask_c, s_c, NEG_INF), axis=-1) * mask_c
    o_c = jnp.einsum('bgrsc,bgcd->bgrsd', p_c.astype(dt), vc)

    ns = S // SEL_BLOCK
    top_n = min(SEL_TOPN, ns)
    cs = np.arange(nc)[:, None] * CMP_STRIDE
    js = np.arange(ns)[None, :] * SEL_BLOCK
    overlap = np.clip(np.minimum(cs + CMP_LEN, js + SEL_BLOCK) - np.maximum(cs, js), 0, None)
    overlap = jnp.asarray(overlap / CMP_LEN, dtype=jnp.float32)
    imp = jnp.einsum('bgrsc,cj->bgsj', p_c, overlap)
    cur_blk = t_pos // SEL_BLOCK
    jb = jnp.arange(ns)
    valid = jb[None, :] <= cur_blk[:, None]
    forced = (jb[None, :] == cur_blk[:, None]) | (jb[None, :] == 0)
    imp = jnp.where(forced, jnp.inf, jnp.where(valid, imp, -jnp.inf))
    _, sel_idx = lax.top_k(imp, top_n)

    ks_blk = rmsnorm(to_groups(ks_raw), g_k[1]).reshape(B, G, ns, SEL_BLOCK, DH)
    vs_blk = to_groups(vs_raw).reshape(B, G, ns, SEL_BLOCK, DH)
    nqc = S // SEL_QCHUNK
    q_ch = qn.reshape(B, G, R, nqc, SEL_QCHUNK, DH).transpose(3, 0, 1, 2, 4, 5)
    i_ch = sel_idx.reshape(B, G, nqc, SEL_QCHUNK, top_n).transpose(2, 0, 1, 3, 4)
    gather = jax.vmap(jax.vmap(lambda blocks, ix: blocks[ix]))
    n_keys = top_n * SEL_BLOCK

    def sel_chunk(args):
        qb, ib, ci = args
        kg = gather(ks_blk, ib).reshape(B, G, SEL_QCHUNK, n_keys, DH)
        vg = gather(vs_blk, ib).reshape(B, G, SEL_QCHUNK, n_keys, DH)
        kpos = (ib[..., None] * SEL_BLOCK + jnp.arange(SEL_BLOCK)).reshape(B, G, SEL_QCHUNK, n_keys)
        tq = ci * SEL_QCHUNK + jnp.arange(SEL_QCHUNK)
        m = kpos <= tq[None, None, :, None]
        s = jnp.einsum('bgrqd,bgqkd->bgrqk', qb, kg).astype(jnp.float32) * scale
        p = jax.nn.softmax(jnp.where(m[:, :, None], s, NEG_INF), axis=-1)
        return jnp.einsum('bgrqk,bgqkd->bgrqd', p.astype(dt), vg)

    o_s = lax.map(sel_chunk, (q_ch, i_ch, jnp.arange(nqc)))
    o_s = o_s.transpose(1, 2, 3, 0, 4, 5).reshape(B, G, R, S, DH)

    nqb = S // WIN_QBLOCK
    n_band = WINDOW // WIN_QBLOCK + 1
    band = n_band * WIN_QBLOCK

    def banded(t):
        tp = jnp.pad(t, ((0, 0), (0, 0), (WINDOW, 0), (0, 0)))
        tp = tp.reshape(B, G, (S + WINDOW) // WIN_QBLOCK, WIN_QBLOCK, DH)
        bands = jnp.concatenate([tp[:, :, i:i + nqb] for i in range(n_band)], axis=3)
        return bands.transpose(2, 0, 1, 3, 4)

    kw_band = banded(rmsnorm(to_groups(kw_raw), g_k[2]))
    vw_band = banded(to_groups(vw_raw))
    qi = jnp.arange(WIN_QBLOCK)[:, None]
    mi = jnp.arange(band)[None, :]
    rel_ok = (mi >= qi + 1) & (mi <= qi + WINDOW)
    pos_ok = (jnp.arange(nqb)[:, None] * WIN_QBLOCK - WINDOW + jnp.arange(band)[None, :]) >= 0
    mask_w = rel_ok[None] & pos_ok[:, None, :]
    q_wb = qn.reshape(B, G, R, nqb, WIN_QBLOCK, DH).transpose(3, 0, 1, 2, 4, 5)

    def win_block(args):
        qb, kb, vb, mb = args
        s = jnp.einsum('bgrqd,bgkd->bgrqk', qb, kb).astype(jnp.float32) * scale
        p = jax.nn.softmax(jnp.where(mb, s, NEG_INF), axis=-1)
        return jnp.einsum('bgrqk,bgkd->bgrqd', p.astype(dt), vb)

    o_w = lax.map(win_block, (q_wb, kw_band, vw_band, mask_w))
    o_w = o_w.transpose(1, 2, 3, 0, 4, 5).reshape(B, G, R, S, DH)

    gates = jax.nn.sigmoid(gate_logits.astype(jnp.float32)).astype(dt)
    gates = gates.reshape(B, S, G, R, NSA_BRANCHES).transpose(4, 0, 2, 3, 1)[..., None]
    o = gates[0] * o_c + gates[1] * o_s + gates[2] * o_w
    return o.transpose(0, 3, 1, 2, 4).reshape(B, S, NSA_WIDTH)


def hgrn2_mixer(q, f, i, g, lb, g_norm):
    B, S = q.shape[0], q.shape[1]
    H, K, V, C = HGRN_HEADS, HGRN_DK, HGRN_DV, HGRN_CHUNK
    dt = q.dtype

    def heads(t, d):
        return t.reshape(B, S, H, d).transpose(0, 2, 1, 3).astype(jnp.float32)

    lbh = lb.astype(jnp.float32).reshape(H, 1, K)
    qf = jax.nn.silu(heads(q, K))
    fg = lbh + (1.0 - lbh) * jax.nn.sigmoid(heads(f, K))
    kf = 1.0 - fg
    logf = jnp.log(fg)
    vf = heads(i, V)
    nch = S // C

    def chunk(t):
        return t.reshape(B, H, nch, C, t.shape[-1]).transpose(2, 0, 1, 3, 4)

    causal = jnp.tril(jnp.ones((C, C), dtype=bool))[:, :, None]

    def step(state, xs):
        qc, kc, vc, lc = xs
        a = jnp.cumsum(lc, axis=2)
        decay = jnp.exp(jnp.where(causal, a[:, :, :, None, :] - a[:, :, None, :, :], -jnp.inf))
        attn = jnp.einsum('bhtk,bhsk,bhtsk->bhts', qc, kc, decay)
        out = attn @ vc + jnp.einsum('bhtk,bhkv->bhtv', qc * jnp.exp(a), state)
        a_last = a[:, :, -1:, :]
        new_state = jnp.exp(a_last[:, :, 0, :, None]) * state + jnp.einsum('bhsk,bhsv->bhkv', kc * jnp.exp(a_last - a), vc)
        return new_state, out

    _, o = lax.scan(step, jnp.zeros((B, H, K, V), jnp.float32), (chunk(qf), chunk(kf), chunk(vf), chunk(logf)))
    o = o.transpose(1, 3, 0, 2, 4).reshape(B, S // C * 0 + nch, C, H, V) if False else o.transpose(1, 0, 3, 2, 4).reshape(B, S, H, V)
    o = rmsnorm(o, g_norm).reshape(B, S, HGRN_WIDTH) * jax.nn.silu(g.astype(jnp.float32))
    return o.astype(dt)


def hier_moe(x, w_group, b_group, w_expert, b_expert, w1, w3, w2):
    B, S, D = x.shape
    N = B * S
    xt = x.reshape(N, D)
    pg = jax.nn.softmax((xt @ w_group).astype(jnp.float32) + b_group.astype(jnp.float32), axis=-1)
    pg_top, g_idx = lax.top_k(pg, 1)
    le = ((xt @ w_expert).astype(jnp.float32) + b_expert.astype(jnp.float32)).reshape(N, MOE_GROUPS, EXPERTS_PER_GROUP)
    le = jnp.take_along_axis(le, jnp.broadcast_to(g_idx[:, :, None], (N, 1, EXPERTS_PER_GROUP)), axis=1)[:, 0]
    pe = jax.nn.softmax(le, axis=-1)
    pe_top, e_loc = lax.top_k(pe, MOE_TOPK)
    w_tok = pg_top * pe_top / jnp.sum(pe_top, axis=-1, keepdims=True)
    e_id = g_idx * EXPERTS_PER_GROUP + e_loc

    n_asg = N * MOE_TOPK
    e_flat = e_id.reshape(n_asg)
    w_flat = w_tok.reshape(n_asg)
    tok = jnp.repeat(jnp.arange(N), MOE_TOPK)
    order = jnp.argsort(e_flat)
    e_s, tok_s, w_s = e_flat[order], tok[order], w_flat[order]
    counts = jnp.bincount(e_flat, length=N_EXPERTS)
    start = jnp.cumsum(counts) - counts
    padded = (counts + MOE_BLOCK - 1) // MOE_BLOCK * MOE_BLOCK
    pend = jnp.cumsum(padded)
    pstart = pend - padded
    dest = pstart[e_s] + jnp.arange(n_asg) - start[e_s]
    P = n_asg + N_EXPERTS * MOE_BLOCK
    nblk = P // MOE_BLOCK
    x_pad = jnp.zeros((P, D), x.dtype).at[dest].set(xt[tok_s])
    blk_e = jnp.minimum(jnp.sum(jnp.arange(nblk)[:, None] * MOE_BLOCK >= pend[None, :], axis=1), N_EXPERTS - 1)

    def expert_block(args):
        xb, e = args
        return (jax.nn.silu(xb @ w1[e]) * (xb @ w3[e])) @ w2[e]

    y_pad = lax.map(expert_block, (x_pad.reshape(nblk, MOE_BLOCK, D), blk_e)).reshape(P, D)
    y = jax.ops.segment_sum(y_pad[dest] * w_s[:, None].astype(x.dtype), tok_s, num_segments=N)
    return y.reshape(B, S, D)


def setup_inputs(seed: int = 0) -> dict:
    key = jax.random.key(seed)
    ks = jax.random.split(key, 24)
    nrm = lambda k, shape, s: jax.random.normal(k, shape, jnp.float32) * s
    DH = NSA_HEAD_DIM
    return {
        'x': nrm(ks[0], (BATCH, SEQ, D_MODEL), 1.0),
        'attn_norm_g': 1.0 + nrm(ks[1], (DEPTH, D_MODEL), 0.02),
        'ffn_norm_g': 1.0 + nrm(ks[2], (DEPTH, D_MODEL), 0.02),
        'w_in': nrm(ks[3], (DEPTH, D_MODEL, IN_COLS), D_MODEL ** -0.5),
        'w_out': nrm(ks[4], (DEPTH, MIX_WIDTH, D_MODEL), MIX_WIDTH ** -0.5),
        'nsa_q_norm_g': 1.0 + nrm(ks[5], (DEPTH, DH), 0.02),
        'nsa_k_norm_g': 1.0 + nrm(ks[6], (DEPTH, NSA_BRANCHES, DH), 0.02),
        'cmp_pos_k': nrm(ks[7], (DEPTH, CMP_LEN, DH), 0.1),
        'cmp_pos_v': nrm(ks[8], (DEPTH, CMP_LEN, DH), 0.1),
        'cmp_wk1': nrm(ks[9], (DEPTH, CMP_LEN * DH, CMP_HIDDEN), (CMP_LEN * DH) ** -0.5),
        'cmp_wk2': nrm(ks[10], (DEPTH, CMP_HIDDEN, DH), CMP_HIDDEN ** -0.5),
        'cmp_wv1': nrm(ks[11], (DEPTH, CMP_LEN * DH, CMP_HIDDEN), (CMP_LEN * DH) ** -0.5),
        'cmp_wv2': nrm(ks[12], (DEPTH, CMP_HIDDEN, DH), CMP_HIDDEN ** -0.5),
        'nsa_out_norm_g': 1.0 + nrm(ks[13], (DEPTH, NSA_WIDTH), 0.02),
        'hgrn_lb_logits': nrm(ks[14], (DEPTH + 1, HGRN_WIDTH), 0.5),
        'hgrn_out_norm_g': 1.0 + nrm(ks[15], (DEPTH, HGRN_DV), 0.02),
        'moe_w_group': nrm(ks[16], (DEPTH, D_MODEL, MOE_GROUPS), D_MODEL ** -0.5),
        'moe_b_group': nrm(ks[17], (DEPTH, MOE_GROUPS), 0.01),
        'moe_w_expert': nrm(ks[18], (DEPTH, D_MODEL, N_EXPERTS), D_MODEL ** -0.5),
        'moe_b_expert': nrm(ks[19], (DEPTH, N_EXPERTS), 0.01),
        'moe_w1': nrm(ks[20], (DEPTH, N_EXPERTS, D_MODEL, EXPERT_FF), D_MODEL ** -0.5),
        'moe_w3': nrm(ks[21], (DEPTH, N_EXPERTS, D_MODEL, EXPERT_FF), D_MODEL ** -0.5),
        'moe_w2': nrm(ks[22], (DEPTH, N_EXPERTS, EXPERT_FF, D_MODEL), EXPERT_FF ** -0.5),
    }


def reference(x, attn_norm_g, ffn_norm_g, w_in, w_out, nsa_q_norm_g, nsa_k_norm_g,
              cmp_pos_k, cmp_pos_v, cmp_wk1, cmp_wk2, cmp_wv1, cmp_wv2, nsa_out_norm_g,
              hgrn_lb_logits, hgrn_out_norm_g, moe_w_group, moe_b_group, moe_w_expert,
              moe_b_expert, moe_w1, moe_w3, moe_w2):
    B, S, _ = x.shape
    split_at = np.cumsum(IN_SPLITS)[:-1].tolist()
    lb_all = jnp.cumsum(jax.nn.softmax(hgrn_lb_logits.astype(jnp.float32), axis=0), axis=0)
    h = x
    for layer in range(DEPTH):
        n = rmsnorm(h, attn_norm_g[layer])
        parts = jnp.split(n @ w_in[layer], split_at, axis=-1)
        q = parts[0].reshape(B, S, NSA_HEADS, NSA_HEAD_DIM)
        gate_logits = parts[7].reshape(B, S, NSA_HEADS, NSA_BRANCHES)
        o_nsa = nsa_mixer(q, parts[1], parts[2], parts[3], parts[4], parts[5], parts[6], gate_logits,
                          nsa_q_norm_g[layer], nsa_k_norm_g[layer], cmp_pos_k[layer], cmp_pos_v[layer],
                          cmp_wk1[layer], cmp_wk2[layer], cmp_wv1[layer], cmp_wv2[layer])
        o_nsa = rmsnorm(o_nsa, nsa_out_norm_g[layer])
        o_hgrn = hgrn2_mixer(parts[8], parts[9], parts[10], parts[11], lb_all[layer], hgrn_out_norm_g[layer])
        mixed = jnp.concatenate([o_nsa, o_hgrn.astype(o_nsa.dtype)], axis=-1)
        h = h + mixed @ w_out[layer]
        h = h + hier_moe(rmsnorm(h, ffn_norm_g[layer]), moe_w_group[layer], moe_b_group[layer],
                         moe_w_expert[layer], moe_b_expert[layer], moe_w1[layer], moe_w3[layer], moe_w2[layer])
    return h
```

```python
import functools

import jax
import jax.numpy as jnp
import numpy as np
from jax import lax
from jax.experimental import pallas as pl
from jax.experimental.pallas import tpu as pltpu

F32 = jnp.float32
BF16 = jnp.bfloat16
I32 = jnp.int32

D_MODEL = 1024
NSA_HEADS = 8
NSA_GROUPS = 2
NSA_REP = NSA_HEADS // NSA_GROUPS
HEAD_DIM = 64
NSA_WIDTH = NSA_HEADS * HEAD_DIM
KV_DIM = NSA_GROUPS * HEAD_DIM
NSA_BRANCHES = 3
CMP_STRIDE = 16
CMP_LEN = 32
CMP_HIDDEN = 256
SEL_BLOCK = 64
SEL_TOPN = 16
WINDOW = 512
HGRN_WIDTH = D_MODEL - NSA_WIDTH
HGRN_HEADS = 4
HGRN_DK = 128
HGRN_CHUNK = 64
HGRN_SUB = 16
MOE_GROUPS = 4
EXPERTS_PER_GROUP = 8
N_EXPERTS = MOE_GROUPS * EXPERTS_PER_GROUP
EXPERT_FF = 512
MOE_BLOCK = 256
RMS_EPS = 1e-6
NEG_INF = -1e30

LANES = 128
QKV_COLS = NSA_WIDTH + 6 * KV_DIM
HG_COLS = 4 * HGRN_WIDTH
IN_COLS_PAD = QKV_COLS + HG_COLS + LANES
VMEM_LIMIT = 56 * 1024 * 1024


def _dot(a, b):
    return jnp.dot(a, b, preferred_element_type=F32)


def _dot_nt(a, b):
    return lax.dot_general(a, b, (((1,), (1,)), ((), ())), preferred_element_type=F32)


def _dot_tn(a, b):
    return lax.dot_general(a, b, (((0,), (0,)), ((), ())), preferred_element_type=F32)


def _split3(x):
    a = x.astype(BF16)
    r = x - a.astype(F32)
    b = r.astype(BF16)
    c = (r - b.astype(F32)).astype(BF16)
    return a, b, c


def _silu(x):
    return x * jax.nn.sigmoid(x)


def _params(sem):
    return pltpu.CompilerParams(dimension_semantics=sem, vmem_limit_bytes=VMEM_LIMIT)


def _seg_rms(blk, low, gain):
    sq = blk * blk
    s_lo = jnp.sum(jnp.where(low, sq, 0.0), axis=-1, keepdims=True)
    s_hi = jnp.sum(jnp.where(low, 0.0, sq), axis=-1, keepdims=True)
    inv = jnp.where(low, lax.rsqrt(s_lo * (1.0 / HEAD_DIM) + RMS_EPS),
                    lax.rsqrt(s_hi * (1.0 / HEAD_DIM) + RMS_EPS))
    return blk * inv * gain


def _inproj_body(x_ref, g_ref, w_ref, gq_ref, gk_ref,
                 qpad_ref, ksel_ref, vsel_ref, kwin_ref, vwin_ref, kc_ref, vc_ref,
                 gates_ref, hg_ref, *, tm, seq):
    x = x_ref[...]
    ms = jnp.mean(x * x, axis=-1, keepdims=True)
    n = (x * lax.rsqrt(ms + RMS_EPS) * g_ref[...]).astype(BF16)
    y = _dot(n, w_ref[...])

    lane = lax.broadcasted_iota(I32, (1, LANES), 1)
    low = lane < HEAD_DIM
    gq = gq_ref[...]
    scale = HEAD_DIM ** -0.5
    for j in range(NSA_HEADS // 2):
        nrm = _seg_rms(y[:, LANES * j:LANES * (j + 1)], low, gq) * scale
        qpad_ref[2 * j] = jnp.where(low, nrm, 0.0).astype(BF16)
        qpad_ref[2 * j + 1] = pltpu.roll(jnp.where(low, 0.0, nrm), HEAD_DIM, 1).astype(BF16)

    def kv_block(idx):
        c0 = NSA_WIDTH + LANES * idx
        return y[:, c0:c0 + LANES]

    def split_groups(blk, hi_fill):
        g0 = jnp.where(low, blk, hi_fill)
        g1 = jnp.where(low, pltpu.roll(blk, HEAD_DIM, 1), hi_fill)
        return g0.astype(BF16), g1.astype(BF16)

    kc_ref[...] = kv_block(0).astype(BF16)
    vc_ref[...] = kv_block(1).astype(BF16)

    row = lax.broadcasted_iota(I32, (tm, LANES), 0)
    t = (pl.program_id(0) * tm) % seq + row
    onehot = jnp.where(lane - HEAD_DIM == lax.shift_right_logical(t, 6), 1.0, 0.0)
    ks = _seg_rms(kv_block(2), low, gk_ref[1:2, :])
    ksel_ref[0], ksel_ref[1] = split_groups(ks, onehot)
    vsel_ref[0], vsel_ref[1] = split_groups(kv_block(3), 0.0)
    kw = _seg_rms(kv_block(4), low, gk_ref[2:3, :])
    kwin_ref[0], kwin_ref[1] = split_groups(kw, 0.0)
    vwin_ref[0], vwin_ref[1] = split_groups(kv_block(5), 0.0)

    hg_ref[...] = y[:, QKV_COLS:QKV_COLS + HG_COLS].astype(BF16)
    gates_ref[...] = y[:, QKV_COLS + HG_COLS:]


def _inproj(x2, g_attn, w_in_p, gq128, gk128, *, seq, tm):
    n_tok = x2.shape[0]
    grid = (n_tok // tm,)
    row_spec = lambda cols: pl.BlockSpec((tm, cols), lambda i: (i, 0))
    head_spec = lambda nh: pl.BlockSpec((nh, tm, LANES), lambda i: (0, i, 0))
    full = lambda a: pl.BlockSpec(a.shape, lambda i: (0,) * a.ndim)
    out_shape = (
        jax.ShapeDtypeStruct((NSA_HEADS, n_tok, LANES), BF16),
        jax.ShapeDtypeStruct((NSA_GROUPS, n_tok, LANES), BF16),
        jax.ShapeDtypeStruct((NSA_GROUPS, n_tok, LANES), BF16),
        jax.ShapeDtypeStruct((NSA_GROUPS, n_tok, LANES), BF16),
        jax.ShapeDtypeStruct((NSA_GROUPS, n_tok, LANES), BF16),
        jax.ShapeDtypeStruct((n_tok, LANES), BF16),
        jax.ShapeDtypeStruct((n_tok, LANES), BF16),
        jax.ShapeDtypeStruct((n_tok, LANES), F32),
        jax.ShapeDtypeStruct((n_tok, HG_COLS), BF16),
    )
    out_specs = (head_spec(NSA_HEADS), head_spec(2), head_spec(2), head_spec(2), head_spec(2),
                 row_spec(LANES), row_spec(LANES), row_spec(LANES), row_spec(HG_COLS))
    return pl.pallas_call(
        functools.partial(_inproj_body, tm=tm, seq=seq),
        grid=grid,
        in_specs=[row_spec(D_MODEL), full(g_attn), full(w_in_p), full(gq128), full(gk128)],
        out_specs=out_specs,
        out_shape=out_shape,
        compiler_params=_params(("parallel",)),
        name="inproj",
    )(x2, g_attn, w_in_p, gq128, gk128)


def _compress_body(kc_ref, vc_ref, posk_ref, posv_ref, wk1_ref, wv1_ref, wk2_ref, wv2_ref,
                   gk_ref, kco_ref, vco_ref, *, nrow):
    lane = lax.broadcasted_iota(I32, (1, LANES), 1)
    low = lane < HEAD_DIM

    def mlp(x_ref, pos_ref, w1_ref, w2_ref):
        x = x_ref[0].astype(F32)
        ha = _dot((x + pos_ref[0:1, :]).astype(BF16), w1_ref[0])
        hb = _dot((x + pos_ref[1:2, :]).astype(BF16), w1_ref[1])
        hid = _silu(ha + pltpu.roll(hb, nrow - 1, 0))
        return _dot(hid.astype(BF16), w2_ref[...])

    def split_groups(blk):
        g0 = jnp.where(low, blk, 0.0)
        g1 = jnp.where(low, pltpu.roll(blk, HEAD_DIM, 1), 0.0)
        return g0.astype(BF16), g1.astype(BF16)

    kc = _seg_rms(mlp(kc_ref, posk_ref, wk1_ref, wk2_ref), low, gk_ref[0:1, :])
    kco_ref[0, 0], kco_ref[0, 1] = split_groups(kc)
    vc = mlp(vc_ref, posv_ref, wv1_ref, wv2_ref)
    vco_ref[0, 0], vco_ref[0, 1] = split_groups(vc)


def _compress(kc3, vc3, posk, posv, wk1, wv1, wk2, wv2, gk128):
    batch, nrow, width = kc3.shape
    full = lambda a: pl.BlockSpec(a.shape, lambda b: (0,) * a.ndim)
    in_spec = pl.BlockSpec((1, nrow, width), lambda b: (b, 0, 0))
    out_spec = pl.BlockSpec((1, NSA_GROUPS, nrow, LANES), lambda b: (b, 0, 0, 0))
    out_shape = jax.ShapeDtypeStruct((batch, NSA_GROUPS, nrow, LANES), BF16)
    return pl.pallas_call(
        functools.partial(_compress_body, nrow=nrow),
        grid=(batch,),
        in_specs=[in_spec, in_spec, full(posk), full(posv), full(wk1), full(wv1),
                  full(wk2), full(wv2), full(gk128)],
        out_specs=(out_spec, out_spec),
        out_shape=(out_shape, out_shape),
        compiler_params=_params(("parallel",)),
        name="compress",
    )(kc3, vc3, posk, posv, wk1, wv1, wk2, wv2, gk128)


SEL_CHUNK = 512


def _attn_body(q_ref, kc_ref, vc_ref, ks_ref, vs_ref, kw_ref, vw_ref, gt_ref, o_ref,
               m_sc, l_sc, acc_sc, *, tq, seq, ncmp):
    g = pl.program_id(1)
    t0 = pl.program_id(2) * tq
    rows = NSA_REP * tq
    q = q_ref[...].reshape(rows, LANES)
    trow = t0 + (lax.broadcasted_iota(I32, (rows, 1), 0) & (tq - 1))

    kc = kc_ref[0, 0]
    s = _dot_nt(q, kc)
    cpos = lax.broadcasted_iota(I32, (1, ncmp), 1) * CMP_STRIDE + (CMP_LEN - 1)
    cmask = cpos <= trow
    s = jnp.where(cmask, s, NEG_INF)
    e = jnp.where(cmask, jnp.exp(s - jnp.max(s, axis=-1, keepdims=True)), 0.0)
    den = jnp.sum(e, axis=-1, keepdims=True)
    p_c = e / jnp.where(den > 0.0, den, 1.0)
    o_c = _dot(p_c.astype(BF16), vc_ref[0, 0])

    p_sum = p_c[0:tq]
    for r in range(1, NSA_REP):
        p_sum = p_sum + p_c[r * tq:(r + 1) * tq]
    nblk = seq // SEL_BLOCK
    jj = lax.broadcasted_iota(I32, (nblk, ncmp), 0) * SEL_BLOCK
    cc = lax.broadcasted_iota(I32, (nblk, ncmp), 1) * CMP_STRIDE
    ov = jnp.maximum(jnp.minimum(cc + CMP_LEN, jj + SEL_BLOCK) - jnp.maximum(cc, jj), 0)
    ov_t = (ov.astype(F32) * (1.0 / CMP_LEN)).astype(BF16)
    p1, p2, p3 = _split3(p_sum)
    imp = _dot_nt(ov_t, p1) + _dot_nt(ov_t, p2) + _dot_nt(ov_t, p3)

    jb = lax.broadcasted_iota(I32, (nblk, tq), 0)
    cur = lax.shift_right_logical(t0 + lax.broadcasted_iota(I32, (nblk, tq), 1), 6)
    valid = jb <= cur
    forced = (jb == cur) | (jb == 0)
    imp = jnp.where(forced, jnp.inf, jnp.where(valid, imp, -jnp.inf))
    rank = jnp.zeros((nblk, tq), F32)
    for j2 in range(nblk):
        other = imp[j2:j2 + 1, :]
        ahead = (other > imp) | ((other == imp) & (jb > j2))
        rank = rank + jnp.where(ahead, 1.0, 0.0)
    sel = (rank < float(min(SEL_TOPN, nblk))) & valid
    bias_t = jnp.where(sel, 0.0, NEG_INF)
    pad_rows = LANES - nblk
    bias_full = jnp.concatenate([jnp.zeros((HEAD_DIM, tq), F32), bias_t] +
                                ([jnp.zeros((pad_rows - HEAD_DIM, tq), F32)] if pad_rows > HEAD_DIM else []),
                                axis=0)
    bias = bias_full.T.astype(BF16)
    lane = lax.broadcasted_iota(I32, (1, LANES), 1)
    low = lane < HEAD_DIM
    bias_rows = jnp.concatenate([bias] * NSA_REP, axis=0)
    q_sel = jnp.where(low, q, bias_rows)

    m_sc[...] = jnp.full((rows, 1), -3.0e38, F32)
    l_sc[...] = jnp.zeros((rows, 1), F32)
    acc_sc[...] = jnp.zeros((rows, LANES), F32)

    def sel_step(start, causal):
        k = ks_ref[0, pl.ds(start, SEL_CHUNK), :]
        sc = _dot_nt(q_sel, k)
        if causal:
            kpos = start + lax.broadcasted_iota(I32, (1, SEL_CHUNK), 1)
            sc = jnp.where(kpos <= trow, sc, NEG_INF)
        m_old = m_sc[...]
        m_new = jnp.maximum(m_old, jnp.max(sc, axis=-1, keepdims=True))
        alpha = jnp.exp(m_old - m_new)
        p = jnp.exp(sc - m_new)
        l_sc[...] = alpha * l_sc[...] + jnp.sum(p, axis=-1, keepdims=True)
        acc_sc[...] = alpha * acc_sc[...] + _dot(p.astype(BF16), vs_ref[0, pl.ds(start, SEL_CHUNK), :])
        m_sc[...] = m_new

    n_full = t0 // SEL_CHUNK

    def full_chunk(ci, carry):
        sel_step(pl.multiple_of(ci * SEL_CHUNK, SEL_CHUNK), False)
        return carry

    lax.fori_loop(0, n_full, full_chunk, 0)
    sel_step(pl.multiple_of(n_full * SEL_CHUNK, SEL_CHUNK), True)
    o_s = acc_sc[...] / l_sc[...]

    band = WINDOW + tq
    w0 = pl.multiple_of(jnp.maximum(t0 - WINDOW, 0), tq)
    sw = _dot_nt(q, kw_ref[0, pl.ds(w0, band), :])
    kpos = w0 + lax.broadcasted_iota(I32, (1, band), 1)
    wmask = (kpos <= trow) & (trow - kpos < WINDOW)
    sw = jnp.where(wmask, sw, NEG_INF)
    pw = jnp.exp(sw - jnp.max(sw, axis=-1, keepdims=True))
    o_w = _dot(pw.astype(BF16), vw_ref[0, pl.ds(w0, band), :]) / jnp.sum(pw, axis=-1, keepdims=True)

    sig = jax.nn.sigmoid(gt_ref[...])

    def gate(r, br):
        col = (g * NSA_REP + r) * NSA_BRANCHES + br
        return jnp.sum(jnp.where(lane == col, sig, 0.0), axis=-1, keepdims=True)

    heads = []
    for r in range(NSA_REP):
        sl = slice(r * tq, (r + 1) * tq)
        heads.append(gate(r, 0) * o_c[sl] + gate(r, 1) * o_s[sl] + gate(r, 2) * o_w[sl])
    o_ref[:, 0:LANES] = (heads[0] + pltpu.roll(heads[1], HEAD_DIM, 1)).astype(BF16)
    o_ref[:, LANES:2 * LANES] = (heads[2] + pltpu.roll(heads[3], HEAD_DIM, 1)).astype(BF16)


def _attention(qpad, kcmp, vcmp, ksel, vsel, kwin, vwin, gates, *, batch, seq, tq):
    n_tok = batch * seq
    ncmp = kcmp.shape[2]
    nq = seq // tq
    kv_spec = pl.BlockSpec((1, seq, LANES), lambda b, g, i: (g, b, 0))
    cmp_spec = pl.BlockSpec((1, 1, ncmp, LANES), lambda b, g, i: (b, g, 0, 0))
    rows = NSA_REP * tq
    return pl.pallas_call(
        functools.partial(_attn_body, tq=tq, seq=seq, ncmp=ncmp),
        grid=(batch, NSA_GROUPS, nq),
        in_specs=[
            pl.BlockSpec((NSA_REP, tq, LANES), lambda b, g, i: (g, b * nq + i, 0)),
            cmp_spec, cmp_spec, kv_spec, kv_spec, kv_spec, kv_spec,
            pl.BlockSpec((tq, LANES), lambda b, g, i: (b * nq + i, 0)),
        ],
        out_specs=pl.BlockSpec((tq, NSA_REP * HEAD_DIM), lambda b, g, i: (b * nq + i, g)),
        out_shape=jax.ShapeDtypeStruct((n_tok, NSA_WIDTH), BF16),
        scratch_shapes=[pltpu.VMEM((rows, 1), F32), pltpu.VMEM((rows, 1), F32),
                        pltpu.VMEM((rows, LANES), F32)],
        compiler_params=_params(("parallel", "parallel", "arbitrary")),
        name="nsa_attention",
    )(qpad, kcmp, vcmp, ksel, vsel, kwin, vwin, gates)


def _hgrn_body(q_ref, f_ref, i_ref, g_ref, lb_ref, gn_ref, o_ref, st_ref, *, tc):
    C, SUB = HGRN_CHUNK, HGRN_SUB

    @pl.when(pl.program_id(2) == 0)
    def _():
        st_ref[...] = jnp.zeros_like(st_ref)

    r = lax.broadcasted_iota(I32, (C, C), 0)
    c = lax.broadcasted_iota(I32, (C, C), 1)
    rs, cs = r >> 4, c >> 4
    one = lambda m: jnp.where(m, 1.0, 0.0).astype(BF16)
    sum_mats = jnp.concatenate([one(c <= r), one((c <= r) & (rs == cs)),
                                one((c > r) & (rs == cs)), one(c > r)], axis=0)
    row = lax.broadcasted_iota(I32, (C, 1), 0)
    lb = lb_ref[...]
    gn = gn_ref[...]

    def chunk(ci, carry):
        sl = pl.ds(pl.multiple_of(ci * C, C), C)
        q = _silu(q_ref[sl, :].astype(F32))
        fg = lb + (1.0 - lb) * jax.nn.sigmoid(f_ref[sl, :].astype(F32))
        k = 1.0 - fg
        v = i_ref[sl, :].astype(F32)
        l1, l2, l3 = _split3(jnp.log(fg))
        sums = _dot(sum_mats, l1) + _dot(sum_mats, l2) + _dot(sum_mats, l3)
        a_full = sums[0:C]
        a_sub = sums[C:2 * C]
        b_sub = sums[2 * C:3 * C]
        b_full = sums[3 * C:4 * C]

        out = jnp.sum(q * k, axis=-1, keepdims=True) * v
        for d in range(1, SUB):
            ok = (row & (SUB - 1)) >= d
            arg = jnp.where(ok, a_sub - pltpu.roll(a_sub, d, 0), NEG_INF)
            w = jnp.sum(q * pltpu.roll(k, d, 0) * jnp.exp(arg), axis=-1, keepdims=True)
            out = out + w * pltpu.roll(v, d, 0)

        qd = q * jnp.exp(a_sub)
        kd = k * jnp.exp(b_sub)
        tot1 = a_sub[2 * SUB - 1:2 * SUB, :]
        tot2 = a_sub[3 * SUB - 1:3 * SUB, :]
        q_mid = qd * jnp.where(row >= 3 * SUB, jnp.exp(tot2), 1.0)
        k_mid = kd * jnp.where(row < SUB, jnp.exp(tot1), 1.0)
        a_adj = _dot_nt(qd.astype(BF16), kd.astype(BF16))
        a_mid = _dot_nt(q_mid.astype(BF16), k_mid.astype(BF16))
        a_off = jnp.where(rs == cs + 1, a_adj, jnp.where(rs - cs >= 2, a_mid, 0.0))
        vb = v.astype(BF16)
        out = out + _dot(a_off.astype(BF16), vb)

        st = st_ref[...]
        out = out + _dot_nt((q * jnp.exp(a_full)).astype(BF16), st.astype(BF16))
        k_dec = (k * jnp.exp(b_full)).astype(BF16)
        st_ref[...] = st * jnp.exp(a_full[C - 1:C, :]) + _dot_tn(vb, k_dec)

        ms = jnp.mean(out * out, axis=-1, keepdims=True)
        on = out * lax.rsqrt(ms + RMS_EPS) * gn
        o_ref[sl, :] = (on * _silu(g_ref[sl, :].astype(F32))).astype(BF16)
        return carry

    lax.fori_loop(0, tc // C, chunk, 0)


def _hgrn(hg, lb, gn, *, batch, seq, tc):
    n_tok = batch * seq
    nt = seq // tc
    part = lambda p: pl.BlockSpec((tc, HGRN_DK), lambda b, h, i, p=p: (b * nt + i, p * HGRN_HEADS + h))
    return pl.pallas_call(
        functools.partial(_hgrn_body, tc=tc),
        grid=(batch, HGRN_HEADS, nt),
        in_specs=[part(0), part(1), part(2), part(3),
                  pl.BlockSpec((1, HGRN_DK), lambda b, h, i: (0, h)),
                  pl.BlockSpec((1, HGRN_DK), lambda b, h, i: (0, 0))],
        out_specs=pl.BlockSpec((tc, HGRN_DK), lambda b, h, i: (b * nt + i, h)),
        out_shape=jax.ShapeDtypeStruct((n_tok, HGRN_WIDTH), BF16),
        scratch_shapes=[pltpu.VMEM((HGRN_DK, HGRN_DK), F32)],
        compiler_params=_params(("parallel", "parallel", "arbitrary")),
        name="hgrn2",
    )(hg, hg, hg, hg, lb, gn)


def _outproj_body(on_ref, oh_ref, x_ref, gno_ref, wo1_ref, wo2_ref, gf_ref, wr_ref, br_ref,
                  h_ref, hn_ref, route_ref):
    a = on_ref[...].astype(F32)
    ms = jnp.mean(a * a, axis=-1, keepdims=True)
    an = (a * lax.rsqrt(ms + RMS_EPS) * gno_ref[...]).astype(BF16)
    h = x_ref[...] + _dot(an, wo1_ref[...]) + _dot(oh_ref[...], wo2_ref[...])
    h_ref[...] = h
    ms2 = jnp.mean(h * h, axis=-1, keepdims=True)
    hn = h * lax.rsqrt(ms2 + RMS_EPS) * gf_ref[...]
    hn_ref[...] = hn

    hn_hi = hn.astype(BF16)
    hn_lo = (hn - hn_hi.astype(F32)).astype(BF16)
    lg = (_dot(hn_hi, wr_ref[0]) + _dot(hn_lo, wr_ref[0]) + _dot(hn_hi, wr_ref[1])) + br_ref[...]

    lane = lax.broadcasted_iota(I32, (1, LANES), 1).astype(F32)
    first = lambda hit: jnp.min(jnp.where(hit, lane, 1e9), axis=-1, keepdims=True)
    gmask = lane < MOE_GROUPS
    lgm = jnp.where(gmask, lg, -jnp.inf)
    mg = jnp.max(lgm, axis=-1, keepdims=True)
    pg_top = 1.0 / jnp.sum(jnp.where(gmask, jnp.exp(lgm - mg), 0.0), axis=-1, keepdims=True)
    gidx = first(lgm == mg)
    eloc = lane - MOE_GROUPS
    emask = (eloc >= 0) & (eloc < N_EXPERTS) & (jnp.floor(eloc * (1.0 / EXPERTS_PER_GROUP)) == gidx)
    le1 = jnp.where(emask, lg, -jnp.inf)
    m1 = jnp.max(le1, axis=-1, keepdims=True)
    i1 = first(le1 == m1)
    le2 = jnp.where(lane == i1, -jnp.inf, le1)
    m2 = jnp.max(le2, axis=-1, keepdims=True)
    i2 = first(le2 == m2)
    e2 = jnp.exp(m2 - m1)
    w1 = pg_top / (1.0 + e2)
    w2 = pg_top * e2 / (1.0 + e2)
    route_ref[...] = jnp.where(lane == 0, i1 - MOE_GROUPS,
                     jnp.where(lane == 1, i2 - MOE_GROUPS,
                     jnp.where(lane == 2, w1, jnp.where(lane == 3, w2, 0.0))))


def _outproj(o_nsa, o_hgrn, x2, gno, wo1, wo2, gf, wr, br, *, tm):
    n_tok = x2.shape[0]
    row_spec = lambda cols: pl.BlockSpec((tm, cols), lambda i: (i, 0))
    full = lambda a: pl.BlockSpec(a.shape, lambda i: (0,) * a.ndim)
    return pl.pallas_call(
        _outproj_body,
        grid=(n_tok // tm,),
        in_specs=[row_spec(NSA_WIDTH), row_spec(HGRN_WIDTH), row_spec(D_MODEL), full(gno),
                  full(wo1), full(wo2), full(gf), full(wr), full(br)],
        out_specs=(row_spec(D_MODEL), row_spec(D_MODEL), row_spec(LANES)),
        out_shape=(jax.ShapeDtypeStruct((n_tok, D_MODEL), F32),
                   jax.ShapeDtypeStruct((n_tok, D_MODEL), F32),
                   jax.ShapeDtypeStruct((n_tok, LANES), F32)),
        compiler_params=_params(("parallel",)),
        name="outproj_router",
    )(o_nsa, o_hgrn, x2, gno, wo1, wo2, gf, wr, br)


def _expert_body(be_ref, nb_ref, x_ref, w1_ref, w3_ref, w2_ref, y_ref):
    @pl.when(pl.program_id(0) < nb_ref[0])
    def _():
        x = x_ref[...].astype(BF16)
        hid = _silu(_dot(x, w1_ref[0])) * _dot(x, w3_ref[0])
        y_ref[...] = _dot(hid.astype(BF16), w2_ref[0])

    @pl.when(pl.program_id(0) >= nb_ref[0])
    def _():
        y_ref[...] = jnp.zeros_like(y_ref)


def _experts(blk_e, n_used, x_pad, w1, w3, w2):
    rows = x_pad.shape[0]
    nblk = rows // MOE_BLOCK
    wspec = lambda a: pl.BlockSpec((1,) + a.shape[1:], lambda i, be, nb: (be[i], 0, 0))
    grid_spec = pltpu.PrefetchScalarGridSpec(
        num_scalar_prefetch=2,
        grid=(nblk,),
        in_specs=[pl.BlockSpec((MOE_BLOCK, D_MODEL), lambda i, be, nb: (i, 0)),
                  wspec(w1), wspec(w3), wspec(w2)],
        out_specs=pl.BlockSpec((MOE_BLOCK, D_MODEL), lambda i, be, nb: (i, 0)),
    )
    return pl.pallas_call(
        _expert_body,
        grid_spec=grid_spec,
        out_shape=jax.ShapeDtypeStruct((rows, D_MODEL), F32),
        compiler_params=_params(("arbitrary",)),
        name="experts",
    )(blk_e, n_used, x_pad, w1, w3, w2)


def _pair_groups(w):
    rows, cols = w.shape
    z = jnp.zeros((NSA_GROUPS, rows, NSA_GROUPS, cols), w.dtype)
    for g in range(NSA_GROUPS):
        z = z.at[g, :, g, :].set(w)
    return z


def _compress_weights(w1, w2, pos):
    w1r = w1.reshape(2, CMP_STRIDE, HEAD_DIM, CMP_HIDDEN)
    halves = []
    for half in range(2):
        z = jnp.zeros((CMP_STRIDE, NSA_GROUPS, HEAD_DIM, NSA_GROUPS, CMP_HIDDEN), F32)
        for g in range(NSA_GROUPS):
            z = z.at[:, g, :, g, :].set(w1r[half])
        halves.append(z.reshape(CMP_STRIDE * KV_DIM, NSA_GROUPS * CMP_HIDDEN))
    w1p = jnp.stack(halves).astype(BF16)
    w2p = _pair_groups(w2).reshape(NSA_GROUPS * CMP_HIDDEN, KV_DIM).astype(BF16)
    posr = pos.reshape(2, CMP_STRIDE, 1, HEAD_DIM)
    posp = jnp.broadcast_to(posr, (2, CMP_STRIDE, NSA_GROUPS, HEAD_DIM)).reshape(2, CMP_STRIDE * KV_DIM)
    return w1p, w2p, posp


def kernel(x, attn_norm_g, ffn_norm_g, w_in, w_out, nsa_q_norm_g, nsa_k_norm_g, cmp_pos_k, cmp_pos_v,
           cmp_wk1, cmp_wk2, cmp_wv1, cmp_wv2, nsa_out_norm_g, hgrn_lb_logits, hgrn_out_norm_g,
           moe_w_group, moe_b_group, moe_w_expert, moe_b_expert, moe_w1, moe_w3, moe_w2):
    batch, seq, _ = x.shape
    n_tok = batch * seq
    assert seq % SEL_CHUNK == 0 and seq >= WINDOW + 128 and seq // SEL_BLOCK <= HEAD_DIM
    depth = w_in.shape[0]
    lb_all = jnp.cumsum(jax.nn.softmax(hgrn_lb_logits.astype(F32), axis=0), axis=0)
    tm = min(512, n_tok)
    h = x.reshape(n_tok, D_MODEL)
    for layer in range(depth):
        wl = w_in[layer]
        gate_cols = wl[:, QKV_COLS:QKV_COLS + NSA_HEADS * NSA_BRANCHES]
        w_in_p = jnp.concatenate(
            [wl[:, :QKV_COLS], wl[:, QKV_COLS + NSA_HEADS * NSA_BRANCHES:],
             jnp.pad(gate_cols, ((0, 0), (0, LANES - NSA_HEADS * NSA_BRANCHES)))], axis=1).astype(BF16)
        gq128 = jnp.tile(nsa_q_norm_g[layer], 2)[None, :]
        gk128 = jnp.tile(nsa_k_norm_g[layer], (1, 2))
        g_attn = attn_norm_g[layer][None, :]

        qpad, ksel, vsel, kwin, vwin, kc_raw, vc_raw, gates, hg = _inproj(
            h, g_attn, w_in_p, gq128, gk128, seq=seq, tm=tm)

        wk1p, wk2p, poskp = _compress_weights(cmp_wk1[layer], cmp_wk2[layer], cmp_pos_k[layer])
        wv1p, wv2p, posvp = _compress_weights(cmp_wv1[layer], cmp_wv2[layer], cmp_pos_v[layer])
        nrow = seq // CMP_STRIDE
        kcmp, vcmp = _compress(kc_raw.reshape(batch, nrow, CMP_STRIDE * KV_DIM),
                               vc_raw.reshape(batch, nrow, CMP_STRIDE * KV_DIM),
                               poskp, posvp, wk1p, wv1p, wk2p, wv2p, gk128)

        o_nsa = _attention(qpad, kcmp, vcmp, ksel, vsel, kwin, vwin, gates,
                           batch=batch, seq=seq, tq=128)
        o_hgrn = _hgrn(hg, lb_all[layer][None, :], hgrn_out_norm_g[layer][None, :],
                       batch=batch, seq=seq, tc=min(512, seq))

        wo = w_out[layer].astype(BF16)
        wr = jnp.concatenate([moe_w_group[layer], moe_w_expert[layer]], axis=1)
        wr = jnp.pad(wr, ((0, 0), (0, LANES - wr.shape[1])))
        wr_hi = wr.astype(BF16)
        wr_lo = (wr - wr_hi.astype(F32)).astype(BF16)
        br = jnp.pad(jnp.concatenate([moe_b_group[layer], moe_b_expert[layer]]),
                     (0, LANES - MOE_GROUPS - N_EXPERTS))[None, :]
        h, hn, route = _outproj(o_nsa, o_hgrn, h, nsa_out_norm_g[layer][None, :],
                                wo[:NSA_WIDTH], wo[NSA_WIDTH:], ffn_norm_g[layer][None, :],
                                jnp.stack([wr_hi, wr_lo]), br, tm=tm)

        e_flat = route[:, 0:2].astype(I32).reshape(-1)
        w_flat = route[:, 2:4].reshape(-1)
        n_asg = e_flat.shape[0]
        onehot = (e_flat[:, None] == jnp.arange(N_EXPERTS)[None, :]).astype(I32)
        rank = jnp.take_along_axis(jnp.cumsum(onehot, axis=0) - onehot, e_flat[:, None], axis=1)[:, 0]
        counts = jnp.sum(onehot, axis=0)
        padded = (counts + MOE_BLOCK - 1) // MOE_BLOCK * MOE_BLOCK
        pend = jnp.cumsum(padded)
        dest = (pend - padded)[e_flat] + rank
        p_rows = n_asg + N_EXPERTS * MOE_BLOCK
        nblk = p_rows // MOE_BLOCK
        tok = jnp.arange(n_asg) // 2
        x_pad = jnp.zeros((p_rows, D_MODEL), F32).at[dest].set(hn[tok])
        blk_e = jnp.minimum(jnp.sum(jnp.arange(nblk)[:, None] * MOE_BLOCK >= pend[None, :], axis=1),
                            N_EXPERTS - 1).astype(I32)
        n_used = (pend[-1] // MOE_BLOCK).astype(I32)[None]
        y_pad = _experts(blk_e, n_used, x_pad, moe_w1[layer].astype(BF16), moe_w3[layer].astype(BF16),
                         moe_w2[layer].astype(BF16))
        y = (y_pad[dest] * w_flat[:, None]).reshape(n_tok, 2, D_MODEL)
        h = h + y[:, 0] + y[:, 1]
    return h.reshape(batch, seq, D_MODEL)
```

```python
import functools

import jax
import jax.numpy as jnp
import numpy as np
from jax import lax
from jax.experimental import pallas as pl
from jax.experimental.pallas import tpu as pltpu

F32 = jnp.float32
BF16 = jnp.bfloat16
I32 = jnp.int32

D_MODEL = 1024
NSA_HEADS = 8
NSA_GROUPS = 2
NSA_REP = NSA_HEADS // NSA_GROUPS
HEAD_DIM = 64
NSA_WIDTH = NSA_HEADS * HEAD_DIM
KV_DIM = NSA_GROUPS * HEAD_DIM
NSA_BRANCHES = 3
CMP_STRIDE = 16
CMP_LEN = 32
CMP_HIDDEN = 256
SEL_BLOCK = 64
SEL_TOPN = 16
WINDOW = 512
HGRN_WIDTH = D_MODEL - NSA_WIDTH
HGRN_HEADS = 4
HGRN_DK = 128
HGRN_CHUNK = 64
HGRN_SUB = 16
MOE_GROUPS = 4
EXPERTS_PER_GROUP = 8
N_EXPERTS = MOE_GROUPS * EXPERTS_PER_GROUP
EXPERT_FF = 512
MOE_BLOCK = 256
RMS_EPS = 1e-6
NEG_INF = -1e30

LANES = 128
QKV_COLS = NSA_WIDTH + 6 * KV_DIM
HG_COLS = 4 * HGRN_WIDTH
IN_COLS_PAD = QKV_COLS + HG_COLS + LANES
GATE_ROWS = 32
VT_BLOCK = 128
LOG2E = 1.4426950408889634
VMEM_LIMIT = 56 * 1024 * 1024


def _dot(a, b):
    return jnp.dot(a, b, preferred_element_type=F32)


def _dot_nt(a, b):
    return lax.dot_general(a, b, (((1,), (1,)), ((), ())), preferred_element_type=F32)


def _dot_tn(a, b):
    return lax.dot_general(a, b, (((0,), (0,)), ((), ())), preferred_element_type=F32)


def _split3(x):
    a = x.astype(BF16)
    r = x - a.astype(F32)
    b = r.astype(BF16)
    c = (r - b.astype(F32)).astype(BF16)
    return a, b, c


def _silu(x):
    return x * jax.nn.sigmoid(x)


def _params(sem):
    return pltpu.CompilerParams(dimension_semantics=sem, vmem_limit_bytes=VMEM_LIMIT)


def _seg_rms(blk, low, gain):
    sq = blk * blk
    s_lo = jnp.sum(jnp.where(low, sq, 0.0), axis=-1, keepdims=True)
    s_hi = jnp.sum(jnp.where(low, 0.0, sq), axis=-1, keepdims=True)
    inv = jnp.where(low, lax.rsqrt(s_lo * (1.0 / HEAD_DIM) + RMS_EPS),
                    lax.rsqrt(s_hi * (1.0 / HEAD_DIM) + RMS_EPS))
    return blk * inv * gain


def _inproj_body(x_ref, g_ref, w_ref, gq_ref, gk_ref,
                 qt_ref, ksel_ref, vselt_ref, kwin_ref, vwint_ref, kc_ref, vc_ref,
                 gates_ref, hg_ref, *, tm, seq):
    x = x_ref[...]
    ms = jnp.mean(x * x, axis=-1, keepdims=True)
    n = (x * lax.rsqrt(ms + RMS_EPS) * g_ref[...]).astype(BF16)
    y = _dot(n, w_ref[...])

    lane = lax.broadcasted_iota(I32, (1, LANES), 1)
    low = lane < HEAD_DIM
    gq = gq_ref[...]
    scale = HEAD_DIM ** -0.5 * LOG2E
    for j in range(NSA_HEADS // 2):
        nrm_t = (_seg_rms(y[:, LANES * j:LANES * (j + 1)], low, gq) * scale).T.astype(BF16)
        qt_ref[2 * j] = nrm_t[0:HEAD_DIM]
        qt_ref[2 * j + 1] = nrm_t[HEAD_DIM:]

    def kv_block(idx):
        c0 = NSA_WIDTH + LANES * idx
        return y[:, c0:c0 + LANES]

    def split_groups(blk, hi_fill):
        g0 = jnp.where(low, blk, hi_fill)
        g1 = jnp.where(low, pltpu.roll(blk, HEAD_DIM, 1), hi_fill)
        return g0.astype(BF16), g1.astype(BF16)

    def store_transposed(vt_ref, blk):
        blk_t = blk.T.astype(BF16)
        for g in range(NSA_GROUPS):
            for c in range(tm // VT_BLOCK):
                vt_ref[g, c] = blk_t[g * HEAD_DIM:(g + 1) * HEAD_DIM, c * VT_BLOCK:(c + 1) * VT_BLOCK]

    kc_ref[...] = kv_block(0).astype(BF16)
    vc_ref[...] = kv_block(1).astype(BF16)

    row = lax.broadcasted_iota(I32, (tm, LANES), 0)
    t = (pl.program_id(0) * tm) % seq + row
    onehot = jnp.where(lane - HEAD_DIM == lax.shift_right_logical(t, 6), 1.0, 0.0)
    ks = _seg_rms(kv_block(2), low, gk_ref[1:2, :])
    ksel_ref[0], ksel_ref[1] = split_groups(ks, onehot)
    store_transposed(vselt_ref, kv_block(3))
    kw = _seg_rms(kv_block(4), low, gk_ref[2:3, :])
    kwin_ref[0], kwin_ref[1] = split_groups(kw, 0.0)
    store_transposed(vwint_ref, kv_block(5))

    hg_ref[...] = y[:, QKV_COLS:QKV_COLS + HG_COLS].astype(BF16)
    gates_ref[...] = y[:, QKV_COLS + HG_COLS:].T[0:GATE_ROWS]


def _inproj(x2, g_attn, w_in_p, gq128, gk128, *, seq, tm):
    n_tok = x2.shape[0]
    grid = (n_tok // tm,)
    row_spec = lambda cols: pl.BlockSpec((tm, cols), lambda i: (i, 0))
    k_spec = pl.BlockSpec((NSA_GROUPS, tm, LANES), lambda i: (0, i, 0))
    vt_spec = pl.BlockSpec((NSA_GROUPS, tm // VT_BLOCK, HEAD_DIM, VT_BLOCK), lambda i: (0, i, 0, 0))
    vt_shape = jax.ShapeDtypeStruct((NSA_GROUPS, n_tok // VT_BLOCK, HEAD_DIM, VT_BLOCK), BF16)
    full = lambda a: pl.BlockSpec(a.shape, lambda i: (0,) * a.ndim)
    out_shape = (
        jax.ShapeDtypeStruct((NSA_HEADS, HEAD_DIM, n_tok), BF16),
        jax.ShapeDtypeStruct((NSA_GROUPS, n_tok, LANES), BF16),
        vt_shape,
        jax.ShapeDtypeStruct((NSA_GROUPS, n_tok, LANES), BF16),
        vt_shape,
        jax.ShapeDtypeStruct((n_tok, LANES), BF16),
        jax.ShapeDtypeStruct((n_tok, LANES), BF16),
        jax.ShapeDtypeStruct((GATE_ROWS, n_tok), F32),
        jax.ShapeDtypeStruct((n_tok, HG_COLS), BF16),
    )
    out_specs = (pl.BlockSpec((NSA_HEADS, HEAD_DIM, tm), lambda i: (0, 0, i)),
                 k_spec, vt_spec, k_spec, vt_spec, row_spec(LANES), row_spec(LANES),
                 pl.BlockSpec((GATE_ROWS, tm), lambda i: (0, i)), row_spec(HG_COLS))
    return pl.pallas_call(
        functools.partial(_inproj_body, tm=tm, seq=seq),
        grid=grid,
        in_specs=[row_spec(D_MODEL), full(g_attn), full(w_in_p), full(gq128), full(gk128)],
        out_specs=out_specs,
        out_shape=out_shape,
        compiler_params=_params(("parallel",)),
        name="inproj",
    )(x2, g_attn, w_in_p, gq128, gk128)


def _compress_body(kc_ref, vc_ref, posk_ref, posv_ref, wk1_ref, wv1_ref, wk2_ref, wv2_ref,
                   gk_ref, kco_ref, vco_ref, *, nrow):
    lane = lax.broadcasted_iota(I32, (1, LANES), 1)
    low = lane < HEAD_DIM

    def mlp(x_ref, pos_ref, w1_ref, w2_ref):
        x = x_ref[0].astype(F32)
        ha = _dot((x + pos_ref[0:1, :]).astype(BF16), w1_ref[0])
        hb = _dot((x + pos_ref[1:2, :]).astype(BF16), w1_ref[1])
        hid = _silu(ha + pltpu.roll(hb, nrow - 1, 0))
        return _dot(hid.astype(BF16), w2_ref[...])

    def split_groups(blk):
        g0 = jnp.where(low, blk, 0.0)
        g1 = jnp.where(low, pltpu.roll(blk, HEAD_DIM, 1), 0.0)
        return g0.astype(BF16), g1.astype(BF16)

    kc = _seg_rms(mlp(kc_ref, posk_ref, wk1_ref, wk2_ref), low, gk_ref[0:1, :])
    kco_ref[0, 0], kco_ref[0, 1] = split_groups(kc)
    vc_t = mlp(vc_ref, posv_ref, wv1_ref, wv2_ref).T.astype(BF16)
    vco_ref[0, 0] = vc_t[0:HEAD_DIM]
    vco_ref[0, 1] = vc_t[HEAD_DIM:]


def _compress(kc3, vc3, posk, posv, wk1, wv1, wk2, wv2, gk128):
    batch, nrow, width = kc3.shape
    full = lambda a: pl.BlockSpec(a.shape, lambda b: (0,) * a.ndim)
    in_spec = pl.BlockSpec((1, nrow, width), lambda b: (b, 0, 0))
    k_spec = pl.BlockSpec((1, NSA_GROUPS, nrow, LANES), lambda b: (b, 0, 0, 0))
    vt_spec = pl.BlockSpec((1, NSA_GROUPS, HEAD_DIM, nrow), lambda b: (b, 0, 0, 0))
    return pl.pallas_call(
        functools.partial(_compress_body, nrow=nrow),
        grid=(batch,),
        in_specs=[in_spec, in_spec, full(posk), full(posv), full(wk1), full(wv1),
                  full(wk2), full(wv2), full(gk128)],
        out_specs=(k_spec, vt_spec),
        out_shape=(jax.ShapeDtypeStruct((batch, NSA_GROUPS, nrow, LANES), BF16),
                   jax.ShapeDtypeStruct((batch, NSA_GROUPS, HEAD_DIM, nrow), BF16)),
        compiler_params=_params(("parallel",)),
        name="compress",
    )(kc3, vc3, posk, posv, wk1, wv1, wk2, wv2, gk128)


SEL_CHUNK = 512


def _vt_cols(vt_ref, start, n):
    b0 = start // VT_BLOCK
    return jnp.concatenate([vt_ref[0, b0 + i] for i in range(n // VT_BLOCK)], axis=1)


def _attn_body(qt_ref, kc_ref, vct_ref, ks_ref, vst_ref, kw_ref, vwt_ref, gt_ref, o_ref,
               *, tq, seq, ncmp):
    g = pl.program_id(1)
    t0 = pl.program_id(2) * tq
    cols = NSA_REP * tq
    qt = jnp.concatenate([qt_ref[r] for r in range(NSA_REP)], axis=1)
    q_pad = jnp.concatenate([qt, jnp.zeros((HEAD_DIM, cols), BF16)], axis=0)
    tcol = t0 + (lax.broadcasted_iota(I32, (1, cols), 1) & (tq - 1))
    tlane = t0 + lax.broadcasted_iota(I32, (1, tq), 1)
    per_head = lambda a: jnp.concatenate([a] * NSA_REP, axis=1)

    s = _dot(kc_ref[0, 0], q_pad)
    cpos = lax.broadcasted_iota(I32, (ncmp, 1), 0) * CMP_STRIDE + (CMP_LEN - 1)
    cmask = cpos <= tcol
    s = jnp.where(cmask, s, NEG_INF)
    e = jnp.where(cmask, jnp.exp2(s - jnp.max(s, axis=0, keepdims=True)), 0.0)
    den = jnp.sum(e, axis=0, keepdims=True)
    p_c = e / jnp.where(den > 0.0, den, 1.0)
    o_c = _dot(vct_ref[0, 0], p_c.astype(BF16))

    p_sum = p_c[:, 0:tq]
    for r in range(1, NSA_REP):
        p_sum = p_sum + p_c[:, r * tq:(r + 1) * tq]
    nblk = seq // SEL_BLOCK
    jj = lax.broadcasted_iota(I32, (nblk, ncmp), 0) * SEL_BLOCK
    cc = lax.broadcasted_iota(I32, (nblk, ncmp), 1) * CMP_STRIDE
    ov = jnp.maximum(jnp.minimum(cc + CMP_LEN, jj + SEL_BLOCK) - jnp.maximum(cc, jj), 0)
    ov = (ov.astype(F32) * (1.0 / CMP_LEN)).astype(BF16)
    p1, p2, p3 = _split3(p_sum)
    imp = _dot(ov, p1) + _dot(ov, p2) + _dot(ov, p3)

    jb = lax.broadcasted_iota(I32, (nblk, tq), 0)
    cur = lax.shift_right_logical(t0 + lax.broadcasted_iota(I32, (nblk, tq), 1), 6)
    valid = jb <= cur
    forced = (jb == cur) | (jb == 0)
    imp = jnp.where(forced, jnp.inf, jnp.where(valid, imp, -jnp.inf))
    rank = jnp.zeros((nblk, tq), F32)
    for j2 in range(nblk):
        other = imp[j2:j2 + 1, :]
        ahead = (other > imp) | ((other == imp) & (jb > j2))
        rank = rank + jnp.where(ahead, 1.0, 0.0)
    sel = (rank < float(min(SEL_TOPN, nblk))) & valid
    bias = jnp.where(sel, 0.0, NEG_INF).astype(BF16)
    if nblk < HEAD_DIM:
        bias = jnp.concatenate([bias, jnp.zeros((HEAD_DIM - nblk, tq), BF16)], axis=0)
    q_sel = jnp.concatenate([qt, per_head(bias)], axis=0)

    def sel_step(start, extra, carry):
        m_old, l_old, acc = carry
        sc = _dot(ks_ref[0, pl.ds(start, SEL_CHUNK), :], q_sel)
        if extra is not None:
            sc = sc + extra
        m_new = jnp.maximum(m_old, jnp.max(sc, axis=0, keepdims=True))
        alpha = jnp.exp2(m_old - m_new)
        p = jnp.exp2(sc - m_new)
        l_new = alpha * l_old + jnp.sum(p, axis=0, keepdims=True)
        acc = alpha * acc + _dot(_vt_cols(vst_ref, start, SEL_CHUNK), p.astype(BF16))
        return m_new, l_new, acc

    n_full = t0 // SEL_CHUNK
    init = (jnp.full((1, cols), -3.0e38, F32), jnp.zeros((1, cols), F32), jnp.zeros((HEAD_DIM, cols), F32))
    carry = lax.fori_loop(
        0, n_full, lambda ci, c: sel_step(pl.multiple_of(ci * SEL_CHUNK, SEL_CHUNK), None, c), init)
    tail0 = pl.multiple_of(n_full * SEL_CHUNK, SEL_CHUNK)
    kpos = tail0 + lax.broadcasted_iota(I32, (SEL_CHUNK, 1), 0)
    causal = jnp.where(kpos <= tlane, 0.0, NEG_INF)
    _, l_s, acc_s = sel_step(tail0, per_head(causal), carry)
    o_s = acc_s / l_s

    band = WINDOW + tq
    w0 = pl.multiple_of(jnp.maximum(t0 - WINDOW, 0), tq)
    sw = _dot(kw_ref[0, pl.ds(w0, band), :], q_pad)
    dist = tlane - (w0 + lax.broadcasted_iota(I32, (band, 1), 0))
    in_win = lax.bitcast_convert_type(dist, jnp.uint32) < jnp.uint32(WINDOW)
    sw = sw + per_head(jnp.where(in_win, 0.0, NEG_INF))
    pw = jnp.exp2(sw - jnp.max(sw, axis=0, keepdims=True))
    o_w = _dot(_vt_cols(vwt_ref, w0, band), pw.astype(BF16)) / jnp.sum(pw, axis=0, keepdims=True)

    def gate(r, br):
        col = (g * NSA_REP + r) * NSA_BRANCHES + br
        return jax.nn.sigmoid(gt_ref[pl.ds(col, 1), :])

    heads = []
    for r in range(NSA_REP):
        sl = slice(r * tq, (r + 1) * tq)
        heads.append(gate(r, 0) * o_c[:, sl] + gate(r, 1) * o_s[:, sl] + gate(r, 2) * o_w[:, sl])
    o_ref[...] = jnp.concatenate(heads, axis=0).T.astype(BF16)


def _attention(qt, kcmp, vcmpt, ksel, vselt, kwin, vwint, gates_t, *, batch, seq, tq):
    n_tok = batch * seq
    ncmp = kcmp.shape[2]
    nq = seq // tq
    nvb = seq // VT_BLOCK
    k_spec = pl.BlockSpec((1, seq, LANES), lambda b, g, i: (g, b, 0))
    vt_spec = pl.BlockSpec((1, nvb, HEAD_DIM, VT_BLOCK), lambda b, g, i: (g, b, 0, 0))
    return pl.pallas_call(
        functools.partial(_attn_body, tq=tq, seq=seq, ncmp=ncmp),
        grid=(batch, NSA_GROUPS, nq),
        in_specs=[
            pl.BlockSpec((NSA_REP, HEAD_DIM, tq), lambda b, g, i: (g, 0, b * nq + i)),
            pl.BlockSpec((1, 1, ncmp, LANES), lambda b, g, i: (b, g, 0, 0)),
            pl.BlockSpec((1, 1, HEAD_DIM, ncmp), lambda b, g, i: (b, g, 0, 0)),
            k_spec, vt_spec, k_spec, vt_spec,
            pl.BlockSpec((GATE_ROWS, tq), lambda b, g, i: (0, b * nq + i)),
        ],
        out_specs=pl.BlockSpec((tq, NSA_REP * HEAD_DIM), lambda b, g, i: (b * nq + i, g)),
        out_shape=jax.ShapeDtypeStruct((n_tok, NSA_WIDTH), BF16),
        compiler_params=_params(("parallel", "parallel", "arbitrary")),
        name="nsa_attention",
    )(qt, kcmp, vcmpt, ksel, vselt, kwin, vwint, gates_t)


def _hgrn_body(q_ref, f_ref, i_ref, g_ref, lb_ref, gn_ref, o_ref, st_ref, *, tc):
    C, SUB = HGRN_CHUNK, HGRN_SUB

    @pl.when(pl.program_id(2) == 0)
    def _():
        st_ref[...] = jnp.zeros_like(st_ref)

    r = lax.broadcasted_iota(I32, (C, C), 0)
    c = lax.broadcasted_iota(I32, (C, C), 1)
    rs, cs = r >> 4, c >> 4
    one = lambda m: jnp.where(m, 1.0, 0.0).astype(BF16)
    sum_mats = jnp.concatenate([one(c <= r), one((c <= r) & (rs == cs)),
                                one((c > r) & (rs == cs)), one(c > r)], axis=0)
    row = lax.broadcasted_iota(I32, (C, 1), 0)
    lb = lb_ref[...]
    gn = gn_ref[...]

    def chunk(ci, carry):
        sl = pl.ds(pl.multiple_of(ci * C, C), C)
        q = _silu(q_ref[sl, :].astype(F32))
        fg = lb + (1.0 - lb) * jax.nn.sigmoid(f_ref[sl, :].astype(F32))
        k = 1.0 - fg
        v = i_ref[sl, :].astype(F32)
        l1, l2, l3 = _split3(jnp.log(fg))
        sums = _dot(sum_mats, l1) + _dot(sum_mats, l2) + _dot(sum_mats, l3)
        a_full = sums[0:C]
        a_sub = sums[C:2 * C]
        b_sub = sums[2 * C:3 * C]
        b_full = sums[3 * C:4 * C]

        out = jnp.sum(q * k, axis=-1, keepdims=True) * v
        for d in range(1, SUB):
            ok = (row & (SUB - 1)) >= d
            arg = jnp.where(ok, a_sub - pltpu.roll(a_sub, d, 0), NEG_INF)
            w = jnp.sum(q * pltpu.roll(k, d, 0) * jnp.exp(arg), axis=-1, keepdims=True)
            out = out + w * pltpu.roll(v, d, 0)

        qd = q * jnp.exp(a_sub)
        kd = k * jnp.exp(b_sub)
        tot1 = a_sub[2 * SUB - 1:2 * SUB, :]
        tot2 = a_sub[3 * SUB - 1:3 * SUB, :]
        q_mid = qd * jnp.where(row >= 3 * SUB, jnp.exp(tot2), 1.0)
        k_mid = kd * jnp.where(row < SUB, jnp.exp(tot1), 1.0)
        a_adj = _dot_nt(qd.astype(BF16), kd.astype(BF16))
        a_mid = _dot_nt(q_mid.astype(BF16), k_mid.astype(BF16))
        a_off = jnp.where(rs == cs + 1, a_adj, jnp.where(rs - cs >= 2, a_mid, 0.0))
        vb = v.astype(BF16)
        out = out + _dot(a_off.astype(BF16), vb)

        st = st_ref[...]
        out = out + _dot_nt((q * jnp.exp(a_full)).astype(BF16), st.astype(BF16))
        k_dec = (k * jnp.exp(b_full)).astype(BF16)
        st_ref[...] = st * jnp.exp(a_full[C - 1:C, :]) + _dot_tn(vb, k_dec)

        ms = jnp.mean(out * out, axis=-1, keepdims=True)
        on = out * lax.rsqrt(ms + RMS_EPS) * gn
        o_ref[sl, :] = (on * _silu(g_ref[sl, :].astype(F32))).astype(BF16)
        return carry

    lax.fori_loop(0, tc // C, chunk, 0)


def _hgrn(hg, lb, gn, *, batch, seq, tc):
    n_tok = batch * seq
    nt = seq // tc
    part = lambda p: pl.BlockSpec((tc, HGRN_DK), lambda b, h, i, p=p: (b * nt + i, p * HGRN_HEADS + h))
    return pl.pallas_call(
        functools.partial(_hgrn_body, tc=tc),
        grid=(batch, HGRN_HEADS, nt),
        in_specs=[part(0), part(1), part(2), part(3),
                  pl.BlockSpec((1, HGRN_DK), lambda b, h, i: (0, h)),
                  pl.BlockSpec((1, HGRN_DK), lambda b, h, i: (0, 0))],
        out_specs=pl.BlockSpec((tc, HGRN_DK), lambda b, h, i: (b * nt + i, h)),
        out_shape=jax.ShapeDtypeStruct((n_tok, HGRN_WIDTH), BF16),
        scratch_shapes=[pltpu.VMEM((HGRN_DK, HGRN_DK), F32)],
        compiler_params=_params(("parallel", "parallel", "arbitrary")),
        name="hgrn2",
    )(hg, hg, hg, hg, lb, gn)


def _outproj_body(on_ref, oh_ref, x_ref, gno_ref, wo1_ref, wo2_ref, gf_ref, wr_ref, br_ref,
                  h_ref, hn_ref, route_ref):
    a = on_ref[...].astype(F32)
    ms = jnp.mean(a * a, axis=-1, keepdims=True)
    an = (a * lax.rsqrt(ms + RMS_EPS) * gno_ref[...]).astype(BF16)
    h = x_ref[...] + _dot(an, wo1_ref[...]) + _dot(oh_ref[...], wo2_ref[...])
    h_ref[...] = h
    ms2 = jnp.mean(h * h, axis=-1, keepdims=True)
    hn = h * lax.rsqrt(ms2 + RMS_EPS) * gf_ref[...]
    hn_ref[...] = hn

    hn_hi = hn.astype(BF16)
    hn_lo = (hn - hn_hi.astype(F32)).astype(BF16)
    lg = (_dot(hn_hi, wr_ref[0]) + _dot(hn_lo, wr_ref[0]) + _dot(hn_hi, wr_ref[1])) + br_ref[...]

    lane = lax.broadcasted_iota(I32, (1, LANES), 1).astype(F32)
    first = lambda hit: jnp.min(jnp.where(hit, lane, 1e9), axis=-1, keepdims=True)
    gmask = lane < MOE_GROUPS
    lgm = jnp.where(gmask, lg, -jnp.inf)
    mg = jnp.max(lgm, axis=-1, keepdims=True)
    pg_top = 1.0 / jnp.sum(jnp.where(gmask, jnp.exp(lgm - mg), 0.0), axis=-1, keepdims=True)
    gidx = first(lgm == mg)
    eloc = lane - MOE_GROUPS
    emask = (eloc >= 0) & (eloc < N_EXPERTS) & (jnp.floor(eloc * (1.0 / EXPERTS_PER_GROUP)) == gidx)
    le1 = jnp.where(emask, lg, -jnp.inf)
    m1 = jnp.max(le1, axis=-1, keepdims=True)
    i1 = first(le1 == m1)
    le2 = jnp.where(lane == i1, -jnp.inf, le1)
    m2 = jnp.max(le2, axis=-1, keepdims=True)
    i2 = first(le2 == m2)
    e2 = jnp.exp(m2 - m1)
    w1 = pg_top / (1.0 + e2)
    w2 = pg_top * e2 / (1.0 + e2)
    route_ref[...] = jnp.where(lane == 0, i1 - MOE_GROUPS,
                     jnp.where(lane == 1, i2 - MOE_GROUPS,
                     jnp.where(lane == 2, w1, jnp.where(lane == 3, w2, 0.0))))


def _outproj(o_nsa, o_hgrn, x2, gno, wo1, wo2, gf, wr, br, *, tm):
    n_tok = x2.shape[0]
    row_spec = lambda cols: pl.BlockSpec((tm, cols), lambda i: (i, 0))
    full = lambda a: pl.BlockSpec(a.shape, lambda i: (0,) * a.ndim)
    return pl.pallas_call(
        _outproj_body,
        grid=(n_tok // tm,),
        in_specs=[row_spec(NSA_WIDTH), row_spec(HGRN_WIDTH), row_spec(D_MODEL), full(gno),
                  full(wo1), full(wo2), full(gf), full(wr), full(br)],
        out_specs=(row_spec(D_MODEL), row_spec(D_MODEL), row_spec(LANES)),
        out_shape=(jax.ShapeDtypeStruct((n_tok, D_MODEL), F32),
                   jax.ShapeDtypeStruct((n_tok, D_MODEL), F32),
                   jax.ShapeDtypeStruct((n_tok, LANES), F32)),
        compiler_params=_params(("parallel",)),
        name="outproj_router",
    )(o_nsa, o_hgrn, x2, gno, wo1, wo2, gf, wr, br)


def _expert_body(be_ref, nb_ref, x_ref, w1_ref, w3_ref, w2_ref, y_ref):
    @pl.when(pl.program_id(0) < nb_ref[0])
    def _():
        x = x_ref[...].astype(BF16)
        hid = _silu(_dot(x, w1_ref[0])) * _dot(x, w3_ref[0])
        y_ref[...] = _dot(hid.astype(BF16), w2_ref[0])

    @pl.when(pl.program_id(0) >= nb_ref[0])
    def _():
        y_ref[...] = jnp.zeros_like(y_ref)


def _experts(blk_e, n_used, x_pad, w1, w3, w2):
    rows = x_pad.shape[0]
    nblk = rows // MOE_BLOCK
    wspec = lambda a: pl.BlockSpec((1,) + a.shape[1:], lambda i, be, nb: (be[i], 0, 0))
    grid_spec = pltpu.PrefetchScalarGridSpec(
        num_scalar_prefetch=2,
        grid=(nblk,),
        in_specs=[pl.BlockSpec((MOE_BLOCK, D_MODEL), lambda i, be, nb: (i, 0)),
                  wspec(w1), wspec(w3), wspec(w2)],
        out_specs=pl.BlockSpec((MOE_BLOCK, D_MODEL), lambda i, be, nb: (i, 0)),
    )
    return pl.pallas_call(
        _expert_body,
        grid_spec=grid_spec,
        out_shape=jax.ShapeDtypeStruct((rows, D_MODEL), F32),
        compiler_params=_params(("arbitrary",)),
        name="experts",
    )(blk_e, n_used, x_pad, w1, w3, w2)


def _pair_groups(w):
    rows, cols = w.shape
    z = jnp.zeros((NSA_GROUPS, rows, NSA_GROUPS, cols), w.dtype)
    for g in range(NSA_GROUPS):
        z = z.at[g, :, g, :].set(w)
    return z


def _compress_weights(w1, w2, pos):
    w1r = w1.reshape(2, CMP_STRIDE, HEAD_DIM, CMP_HIDDEN)
    halves = []
    for half in range(2):
        z = jnp.zeros((CMP_STRIDE, NSA_GROUPS, HEAD_DIM, NSA_GROUPS, CMP_HIDDEN), F32)
        for g in range(NSA_GROUPS):
            z = z.at[:, g, :, g, :].set(w1r[half])
        halves.append(z.reshape(CMP_STRIDE * KV_DIM, NSA_GROUPS * CMP_HIDDEN))
    w1p = jnp.stack(halves).astype(BF16)
    w2p = _pair_groups(w2).reshape(NSA_GROUPS * CMP_HIDDEN, KV_DIM).astype(BF16)
    posr = pos.reshape(2, CMP_STRIDE, 1, HEAD_DIM)
    posp = jnp.broadcast_to(posr, (2, CMP_STRIDE, NSA_GROUPS, HEAD_DIM)).reshape(2, CMP_STRIDE * KV_DIM)
    return w1p, w2p, posp


def kernel(x, attn_norm_g, ffn_norm_g, w_in, w_out, nsa_q_norm_g, nsa_k_norm_g, cmp_pos_k, cmp_pos_v,
           cmp_wk1, cmp_wk2, cmp_wv1, cmp_wv2, nsa_out_norm_g, hgrn_lb_logits, hgrn_out_norm_g,
           moe_w_group, moe_b_group, moe_w_expert, moe_b_expert, moe_w1, moe_w3, moe_w2):
    batch, seq, _ = x.shape
    n_tok = batch * seq
    assert seq % SEL_CHUNK == 0 and seq >= WINDOW + 128 and seq // SEL_BLOCK <= HEAD_DIM
    depth = w_in.shape[0]
    lb_all = jnp.cumsum(jax.nn.softmax(hgrn_lb_logits.astype(F32), axis=0), axis=0)
    tm = min(512, n_tok)
    h = x.reshape(n_tok, D_MODEL)
    for layer in range(depth):
        wl = w_in[layer]
        gate_cols = wl[:, QKV_COLS:QKV_COLS + NSA_HEADS * NSA_BRANCHES]
        w_in_p = jnp.concatenate(
            [wl[:, :QKV_COLS], wl[:, QKV_COLS + NSA_HEADS * NSA_BRANCHES:],
             jnp.pad(gate_cols, ((0, 0), (0, LANES - NSA_HEADS * NSA_BRANCHES)))], axis=1).astype(BF16)
        gq128 = jnp.tile(nsa_q_norm_g[layer], 2)[None, :]
        gk128 = jnp.tile(nsa_k_norm_g[layer], (1, 2))
        g_attn = attn_norm_g[layer][None, :]

        qt, ksel, vselt, kwin, vwint, kc_raw, vc_raw, gates_t, hg = _inproj(
            h, g_attn, w_in_p, gq128, gk128, seq=seq, tm=tm)

        wk1p, wk2p, poskp = _compress_weights(cmp_wk1[layer], cmp_wk2[layer], cmp_pos_k[layer])
        wv1p, wv2p, posvp = _compress_weights(cmp_wv1[layer], cmp_wv2[layer], cmp_pos_v[layer])
        nrow = seq // CMP_STRIDE
        kcmp, vcmpt = _compress(kc_raw.reshape(batch, nrow, CMP_STRIDE * KV_DIM),
                                vc_raw.reshape(batch, nrow, CMP_STRIDE * KV_DIM),
                                poskp, posvp, wk1p, wv1p, wk2p, wv2p, gk128)

        o_nsa = _attention(qt, kcmp, vcmpt, ksel, vselt, kwin, vwint, gates_t,
                           batch=batch, seq=seq, tq=256)
        o_hgrn = _hgrn(hg, lb_all[layer][None, :], hgrn_out_norm_g[layer][None, :],
                       batch=batch, seq=seq, tc=min(512, seq))

        wo = w_out[layer].astype(BF16)
        wr = jnp.concatenate([moe_w_group[layer], moe_w_expert[layer]], axis=1)
        wr = jnp.pad(wr, ((0, 0), (0, LANES - wr.shape[1])))
        wr_hi = wr.astype(BF16)
        wr_lo = (wr - wr_hi.astype(F32)).astype(BF16)
        br = jnp.pad(jnp.concatenate([moe_b_group[layer], moe_b_expert[layer]]),
                     (0, LANES - MOE_GROUPS - N_EXPERTS))[None, :]
        h, hn, route = _outproj(o_nsa, o_hgrn, h, nsa_out_norm_g[layer][None, :],
                                wo[:NSA_WIDTH], wo[NSA_WIDTH:], ffn_norm_g[layer][None, :],
                                jnp.stack([wr_hi, wr_lo]), br, tm=tm)

        e_flat = route[:, 0:2].astype(I32).reshape(-1)
        w_flat = route[:, 2:4].reshape(-1)
        n_asg = e_flat.shape[0]
        onehot = (e_flat[:, None] == jnp.arange(N_EXPERTS)[None, :]).astype(I32)
        rank = jnp.take_along_axis(jnp.cumsum(onehot, axis=0) - onehot, e_flat[:, None], axis=1)[:, 0]
        counts = jnp.sum(onehot, axis=0)
        padded = (counts + MOE_BLOCK - 1) // MOE_BLOCK * MOE_BLOCK
        pend = jnp.cumsum(padded)
        dest = (pend - padded)[e_flat] + rank
        p_rows = n_asg + N_EXPERTS * MOE_BLOCK
        nblk = p_rows // MOE_BLOCK
        tok = jnp.arange(n_asg) // 2
        x_pad = jnp.zeros((p_rows, D_MODEL), F32).at[dest].set(hn[tok])
        blk_e = jnp.minimum(jnp.sum(jnp.arange(nblk)[:, None] * MOE_BLOCK >= pend[None, :], axis=1),
                            N_EXPERTS - 1).astype(I32)
        n_used = (pend[-1] // MOE_BLOCK).astype(I32)[None]
        y_pad = _experts(blk_e, n_used, x_pad, moe_w1[layer].astype(BF16), moe_w3[layer].astype(BF16),
                         moe_w2[layer].astype(BF16))
        y = (y_pad[dest] * w_flat[:, None]).reshape(n_tok, 2, D_MODEL)
        h = h + y[:, 0] + y[:, 1]
    return h.reshape(batch, seq, D_MODEL)
```

```python
import functools

import jax
import jax.numpy as jnp
import numpy as np
from jax import lax
from jax.experimental import pallas as pl
from jax.experimental.pallas import tpu as pltpu

F32 = jnp.float32
BF16 = jnp.bfloat16
I32 = jnp.int32

D_MODEL = 1024
NSA_HEADS = 8
NSA_GROUPS = 2
NSA_REP = NSA_HEADS // NSA_GROUPS
HEAD_DIM = 64
NSA_WIDTH = NSA_HEADS * HEAD_DIM
KV_DIM = NSA_GROUPS * HEAD_DIM
NSA_BRANCHES = 3
CMP_STRIDE = 16
CMP_LEN = 32
CMP_HIDDEN = 256
SEL_BLOCK = 64
SEL_TOPN = 16
WINDOW = 512
HGRN_WIDTH = D_MODEL - NSA_WIDTH
HGRN_HEADS = 4
HGRN_DK = 128
HGRN_CHUNK = 64
HGRN_SUB = 16
MOE_GROUPS = 4
EXPERTS_PER_GROUP = 8
N_EXPERTS = MOE_GROUPS * EXPERTS_PER_GROUP
EXPERT_FF = 512
MOE_BLOCK = 256
RMS_EPS = 1e-6
NEG_INF = -1e30

LANES = 128
QKV_COLS = NSA_WIDTH + 6 * KV_DIM
HG_COLS = 4 * HGRN_WIDTH
IN_COLS_PAD = QKV_COLS + HG_COLS + LANES
GATE_ROWS = 32
VT_BLOCK = 128
LOG2E = 1.4426950408889634
VMEM_LIMIT = 56 * 1024 * 1024


def _dot(a, b):
    return jnp.dot(a, b, preferred_element_type=F32)


def _dot_nt(a, b):
    return lax.dot_general(a, b, (((1,), (1,)), ((), ())), preferred_element_type=F32)


def _dot_tn(a, b):
    return lax.dot_general(a, b, (((0,), (0,)), ((), ())), preferred_element_type=F32)


def _split3(x):
    a = x.astype(BF16)
    r = x - a.astype(F32)
    b = r.astype(BF16)
    c = (r - b.astype(F32)).astype(BF16)
    return a, b, c


def _silu(x):
    return x * jax.nn.sigmoid(x)


def _params(sem):
    return pltpu.CompilerParams(dimension_semantics=sem, vmem_limit_bytes=VMEM_LIMIT)


def _seg_rms(blk, low, gain):
    sq = blk * blk
    s_lo = jnp.sum(jnp.where(low, sq, 0.0), axis=-1, keepdims=True)
    s_hi = jnp.sum(jnp.where(low, 0.0, sq), axis=-1, keepdims=True)
    inv = jnp.where(low, lax.rsqrt(s_lo * (1.0 / HEAD_DIM) + RMS_EPS),
                    lax.rsqrt(s_hi * (1.0 / HEAD_DIM) + RMS_EPS))
    return blk * inv * gain


def _inproj_body(x_ref, g_ref, w_ref, gq_ref, gk_ref,
                 qt_ref, ksel_ref, vselt_ref, kwin_ref, vwint_ref, kc_ref, vc_ref,
                 gates_ref, hg_ref, *, tm, seq):
    x = x_ref[...]
    ms = jnp.mean(x * x, axis=-1, keepdims=True)
    n = (x * lax.rsqrt(ms + RMS_EPS) * g_ref[...]).astype(BF16)
    y = _dot(n, w_ref[...])

    lane = lax.broadcasted_iota(I32, (1, LANES), 1)
    low = lane < HEAD_DIM
    gq = gq_ref[...]
    scale = HEAD_DIM ** -0.5 * LOG2E
    for j in range(NSA_HEADS // 2):
        nrm_t = (_seg_rms(y[:, LANES * j:LANES * (j + 1)], low, gq) * scale).T.astype(BF16)
        qt_ref[2 * j] = nrm_t[0:HEAD_DIM]
        qt_ref[2 * j + 1] = nrm_t[HEAD_DIM:]

    def kv_block(idx):
        c0 = NSA_WIDTH + LANES * idx
        return y[:, c0:c0 + LANES]

    def split_groups(blk, hi_fill):
        g0 = jnp.where(low, blk, hi_fill)
        g1 = jnp.where(low, pltpu.roll(blk, HEAD_DIM, 1), hi_fill)
        return g0.astype(BF16), g1.astype(BF16)

    def store_transposed(vt_ref, blk):
        blk_t = blk.T.astype(BF16)
        for g in range(NSA_GROUPS):
            for c in range(tm // VT_BLOCK):
                vt_ref[g, c] = blk_t[g * HEAD_DIM:(g + 1) * HEAD_DIM, c * VT_BLOCK:(c + 1) * VT_BLOCK]

    kc_ref[...] = kv_block(0).astype(BF16)
    vc_ref[...] = kv_block(1).astype(BF16)

    row = lax.broadcasted_iota(I32, (tm, LANES), 0)
    t = (pl.program_id(0) * tm) % seq + row
    onehot = jnp.where(lane - HEAD_DIM == lax.shift_right_logical(t, 6), 1.0, 0.0)
    ks = _seg_rms(kv_block(2), low, gk_ref[1:2, :])
    ksel_ref[0], ksel_ref[1] = split_groups(ks, onehot)
    store_transposed(vselt_ref, kv_block(3))
    kw = _seg_rms(kv_block(4), low, gk_ref[2:3, :])
    kwin_ref[0], kwin_ref[1] = split_groups(kw, 0.0)
    store_transposed(vwint_ref, kv_block(5))

    hg_ref[...] = y[:, QKV_COLS:QKV_COLS + HG_COLS].astype(BF16)
    gates_ref[...] = y[:, QKV_COLS + HG_COLS:].T[0:GATE_ROWS]


def _inproj(x2, g_attn, w_in_p, gq128, gk128, *, seq, tm):
    n_tok = x2.shape[0]
    grid = (n_tok // tm,)
    row_spec = lambda cols: pl.BlockSpec((tm, cols), lambda i: (i, 0))
    k_spec = pl.BlockSpec((NSA_GROUPS, tm, LANES), lambda i: (0, i, 0))
    vt_spec = pl.BlockSpec((NSA_GROUPS, tm // VT_BLOCK, HEAD_DIM, VT_BLOCK), lambda i: (0, i, 0, 0))
    vt_shape = jax.ShapeDtypeStruct((NSA_GROUPS, n_tok // VT_BLOCK, HEAD_DIM, VT_BLOCK), BF16)
    full = lambda a: pl.BlockSpec(a.shape, lambda i: (0,) * a.ndim)
    out_shape = (
        jax.ShapeDtypeStruct((NSA_HEADS, HEAD_DIM, n_tok), BF16),
        jax.ShapeDtypeStruct((NSA_GROUPS, n_tok, LANES), BF16),
        vt_shape,
        jax.ShapeDtypeStruct((NSA_GROUPS, n_tok, LANES), BF16),
        vt_shape,
        jax.ShapeDtypeStruct((n_tok, LANES), BF16),
        jax.ShapeDtypeStruct((n_tok, LANES), BF16),
        jax.ShapeDtypeStruct((GATE_ROWS, n_tok), F32),
        jax.ShapeDtypeStruct((n_tok, HG_COLS), BF16),
    )
    out_specs = (pl.BlockSpec((NSA_HEADS, HEAD_DIM, tm), lambda i: (0, 0, i)),
                 k_spec, vt_spec, k_spec, vt_spec, row_spec(LANES), row_spec(LANES),
                 pl.BlockSpec((GATE_ROWS, tm), lambda i: (0, i)), row_spec(HG_COLS))
    return pl.pallas_call(
        functools.partial(_inproj_body, tm=tm, seq=seq),
        grid=grid,
        in_specs=[row_spec(D_MODEL), full(g_attn), full(w_in_p), full(gq128), full(gk128)],
        out_specs=out_specs,
        out_shape=out_shape,
        compiler_params=_params(("parallel",)),
        name="inproj",
    )(x2, g_attn, w_in_p, gq128, gk128)


def _compress_body(kc_ref, vc_ref, posk_ref, posv_ref, wk1_ref, wv1_ref, wk2_ref, wv2_ref,
                   gk_ref, kco_ref, vco_ref, *, nrow):
    lane = lax.broadcasted_iota(I32, (1, LANES), 1)
    low = lane < HEAD_DIM

    def mlp(x_ref, pos_ref, w1_ref, w2_ref):
        x = x_ref[0].astype(F32)
        ha = _dot((x + pos_ref[0:1, :]).astype(BF16), w1_ref[0])
        hb = _dot((x + pos_ref[1:2, :]).astype(BF16), w1_ref[1])
        hid = _silu(ha + pltpu.roll(hb, nrow - 1, 0))
        return _dot(hid.astype(BF16), w2_ref[...])

    def split_groups(blk):
        g0 = jnp.where(low, blk, 0.0)
        g1 = jnp.where(low, pltpu.roll(blk, HEAD_DIM, 1), 0.0)
        return g0.astype(BF16), g1.astype(BF16)

    kc = _seg_rms(mlp(kc_ref, posk_ref, wk1_ref, wk2_ref), low, gk_ref[0:1, :])
    kco_ref[0, 0], kco_ref[0, 1] = split_groups(kc)
    vc_t = mlp(vc_ref, posv_ref, wv1_ref, wv2_ref).T.astype(BF16)
    vco_ref[0, 0] = vc_t[0:HEAD_DIM]
    vco_ref[0, 1] = vc_t[HEAD_DIM:]


def _compress(kc3, vc3, posk, posv, wk1, wv1, wk2, wv2, gk128):
    batch, nrow, width = kc3.shape
    full = lambda a: pl.BlockSpec(a.shape, lambda b: (0,) * a.ndim)
    in_spec = pl.BlockSpec((1, nrow, width), lambda b: (b, 0, 0))
    k_spec = pl.BlockSpec((1, NSA_GROUPS, nrow, LANES), lambda b: (b, 0, 0, 0))
    vt_spec = pl.BlockSpec((1, NSA_GROUPS, HEAD_DIM, nrow), lambda b: (b, 0, 0, 0))
    return pl.pallas_call(
        functools.partial(_compress_body, nrow=nrow),
        grid=(batch,),
        in_specs=[in_spec, in_spec, full(posk), full(posv), full(wk1), full(wv1),
                  full(wk2), full(wv2), full(gk128)],
        out_specs=(k_spec, vt_spec),
        out_shape=(jax.ShapeDtypeStruct((batch, NSA_GROUPS, nrow, LANES), BF16),
                   jax.ShapeDtypeStruct((batch, NSA_GROUPS, HEAD_DIM, nrow), BF16)),
        compiler_params=_params(("parallel",)),
        name="compress",
    )(kc3, vc3, posk, posv, wk1, wv1, wk2, wv2, gk128)


SEL_CHUNK = 512


def _vt_cols(vt_ref, start, n):
    b0 = start // VT_BLOCK
    return jnp.concatenate([vt_ref[0, b0 + i] for i in range(n // VT_BLOCK)], axis=1)


def _attn_body(qt_ref, kc_ref, vct_ref, ks_ref, vst_ref, kw_ref, vwt_ref, gt_ref, o_ref,
               *, tq, seq, ncmp):
    g = pl.program_id(1)
    t0 = pl.program_id(2) * tq
    cols = NSA_REP * tq
    qt = jnp.concatenate([qt_ref[r] for r in range(NSA_REP)], axis=1)
    q_pad = jnp.concatenate([qt, jnp.zeros((HEAD_DIM, cols), BF16)], axis=0)
    tcol = t0 + (lax.broadcasted_iota(I32, (1, cols), 1) & (tq - 1))
    tlane = t0 + lax.broadcasted_iota(I32, (1, tq), 1)
    per_head = lambda a: jnp.concatenate([a] * NSA_REP, axis=1)

    s = _dot(kc_ref[0, 0], q_pad)
    cpos = lax.broadcasted_iota(I32, (ncmp, 1), 0) * CMP_STRIDE + (CMP_LEN - 1)
    cmask = cpos <= tcol
    s = jnp.where(cmask, s, NEG_INF)
    e = jnp.where(cmask, jnp.exp2(s - jnp.max(s, axis=0, keepdims=True)), 0.0)
    den = jnp.sum(e, axis=0, keepdims=True)
    p_c = e / jnp.where(den > 0.0, den, 1.0)
    o_c = _dot(vct_ref[0, 0], p_c.astype(BF16))

    p_sum = p_c[:, 0:tq]
    for r in range(1, NSA_REP):
        p_sum = p_sum + p_c[:, r * tq:(r + 1) * tq]
    nblk = seq // SEL_BLOCK
    jj = lax.broadcasted_iota(I32, (nblk, ncmp), 0) * SEL_BLOCK
    cc = lax.broadcasted_iota(I32, (nblk, ncmp), 1) * CMP_STRIDE
    ov = jnp.maximum(jnp.minimum(cc + CMP_LEN, jj + SEL_BLOCK) - jnp.maximum(cc, jj), 0)
    ov = (ov.astype(F32) * (1.0 / CMP_LEN)).astype(BF16)
    p1, p2, p3 = _split3(p_sum)
    imp = _dot(ov, p1) + _dot(ov, p2) + _dot(ov, p3)

    jb = lax.broadcasted_iota(I32, (nblk, tq), 0)
    cur = lax.shift_right_logical(t0 + lax.broadcasted_iota(I32, (nblk, tq), 1), 6)
    valid = jb <= cur
    forced = (jb == cur) | (jb == 0)
    imp = jnp.where(forced, jnp.inf, jnp.where(valid, imp, -jnp.inf))
    rank = jnp.zeros((nblk, tq), F32)
    for j2 in range(nblk):
        other = imp[j2:j2 + 1, :]
        ahead = (other > imp) | ((other == imp) & (jb > j2))
        rank = rank + jnp.where(ahead, 1.0, 0.0)
    sel = (rank < float(min(SEL_TOPN, nblk))) & valid
    bias = jnp.where(sel, 0.0, NEG_INF).astype(BF16)
    if nblk < HEAD_DIM:
        bias = jnp.concatenate([bias, jnp.zeros((HEAD_DIM - nblk, tq), BF16)], axis=0)
    q_sel = jnp.concatenate([qt, per_head(bias)], axis=0)

    def sel_step(start, extra, carry):
        m_old, l_old, acc = carry
        sc = _dot(ks_ref[0, pl.ds(start, SEL_CHUNK), :], q_sel)
        if extra is not None:
            sc = sc + extra
        m_new = jnp.maximum(m_old, jnp.max(sc, axis=0, keepdims=True))
        alpha = jnp.exp2(m_old - m_new)
        p = jnp.exp2(sc - m_new)
        l_new = alpha * l_old + jnp.sum(p, axis=0, keepdims=True)
        acc = alpha * acc + _dot(_vt_cols(vst_ref, start, SEL_CHUNK), p.astype(BF16))
        return m_new, l_new, acc

    n_full = t0 // SEL_CHUNK
    init = (jnp.full((1, cols), -3.0e38, F32), jnp.zeros((1, cols), F32), jnp.zeros((HEAD_DIM, cols), F32))
    carry = lax.fori_loop(
        0, n_full, lambda ci, c: sel_step(pl.multiple_of(ci * SEL_CHUNK, SEL_CHUNK), None, c), init)
    tail0 = pl.multiple_of(n_full * SEL_CHUNK, SEL_CHUNK)
    kpos = tail0 + lax.broadcasted_iota(I32, (SEL_CHUNK, 1), 0)
    causal = jnp.where(kpos <= tlane, 0.0, NEG_INF)
    _, l_s, acc_s = sel_step(tail0, per_head(causal), carry)
    o_s = acc_s / l_s

    band = WINDOW + tq
    w0 = pl.multiple_of(jnp.maximum(t0 - WINDOW, 0), tq)
    sw = _dot(kw_ref[0, pl.ds(w0, band), :], q_pad)
    dist = tlane - (w0 + lax.broadcasted_iota(I32, (band, 1), 0))
    in_win = lax.bitcast_convert_type(dist, jnp.uint32) < jnp.uint32(WINDOW)
    sw = sw + per_head(jnp.where(in_win, 0.0, NEG_INF))
    pw = jnp.exp2(sw - jnp.max(sw, axis=0, keepdims=True))
    o_w = _dot(_vt_cols(vwt_ref, w0, band), pw.astype(BF16)) / jnp.sum(pw, axis=0, keepdims=True)

    def gate(r, br):
        col = (g * NSA_REP + r) * NSA_BRANCHES + br
        return jax.nn.sigmoid(gt_ref[pl.ds(col, 1), :])

    heads = []
    for r in range(NSA_REP):
        sl = slice(r * tq, (r + 1) * tq)
        heads.append(gate(r, 0) * o_c[:, sl] + gate(r, 1) * o_s[:, sl] + gate(r, 2) * o_w[:, sl])
    o_ref[...] = jnp.concatenate(heads, axis=0).T.astype(BF16)


def _attention(qt, kcmp, vcmpt, ksel, vselt, kwin, vwint, gates_t, *, batch, seq, tq):
    n_tok = batch * seq
    ncmp = kcmp.shape[2]
    nq = seq // tq
    nvb = seq // VT_BLOCK
    k_spec = pl.BlockSpec((1, seq, LANES), lambda b, g, i: (g, b, 0))
    vt_spec = pl.BlockSpec((1, nvb, HEAD_DIM, VT_BLOCK), lambda b, g, i: (g, b, 0, 0))
    return pl.pallas_call(
        functools.partial(_attn_body, tq=tq, seq=seq, ncmp=ncmp),
        grid=(batch, NSA_GROUPS, nq),
        in_specs=[
            pl.BlockSpec((NSA_REP, HEAD_DIM, tq), lambda b, g, i: (g, 0, b * nq + i)),
            pl.BlockSpec((1, 1, ncmp, LANES), lambda b, g, i: (b, g, 0, 0)),
            pl.BlockSpec((1, 1, HEAD_DIM, ncmp), lambda b, g, i: (b, g, 0, 0)),
            k_spec, vt_spec, k_spec, vt_spec,
            pl.BlockSpec((GATE_ROWS, tq), lambda b, g, i: (0, b * nq + i)),
        ],
        out_specs=pl.BlockSpec((tq, NSA_REP * HEAD_DIM), lambda b, g, i: (b * nq + i, g)),
        out_shape=jax.ShapeDtypeStruct((n_tok, NSA_WIDTH), BF16),
        compiler_params=_params(("parallel", "parallel", "arbitrary")),
        name="nsa_attention",
    )(qt, kcmp, vcmpt, ksel, vselt, kwin, vwint, gates_t)


def _hgrn_body(q_ref, f_ref, i_ref, g_ref, lb_ref, gn_ref, o_ref, st_ref, *, tc):
    C, SUB = HGRN_CHUNK, HGRN_SUB

    @pl.when(pl.program_id(2) == 0)
    def _():
        st_ref[...] = jnp.zeros_like(st_ref)

    r = lax.broadcasted_iota(I32, (C, C), 0)
    c = lax.broadcasted_iota(I32, (C, C), 1)
    rs, cs = r >> 4, c >> 4
    one = lambda m: jnp.where(m, 1.0, 0.0).astype(BF16)
    sum_mats = jnp.concatenate([one(c <= r), one((c <= r) & (rs == cs)),
                                one((c > r) & (rs == cs)), one(c > r)], axis=0)
    row = lax.broadcasted_iota(I32, (C, 1), 0)
    lb = lb_ref[...]
    gn = gn_ref[...]

    def chunk(ci, carry):
        sl = pl.ds(pl.multiple_of(ci * C, C), C)
        q = _silu(q_ref[sl, :].astype(F32))
        fg = lb + (1.0 - lb) * jax.nn.sigmoid(f_ref[sl, :].astype(F32))
        k = 1.0 - fg
        v = i_ref[sl, :].astype(F32)
        l1, l2, l3 = _split3(jnp.log(fg))
        sums = _dot(sum_mats, l1) + _dot(sum_mats, l2) + _dot(sum_mats, l3)
        a_full = sums[0:C]
        a_sub = sums[C:2 * C]
        b_sub = sums[2 * C:3 * C]
        b_full = sums[3 * C:4 * C]

        out = jnp.sum(q * k, axis=-1, keepdims=True) * v
        for d in range(1, SUB):
            ok = (row & (SUB - 1)) >= d
            arg = jnp.where(ok, a_sub - pltpu.roll(a_sub, d, 0), NEG_INF)
            w = jnp.sum(q * pltpu.roll(k, d, 0) * jnp.exp(arg), axis=-1, keepdims=True)
            out = out + w * pltpu.roll(v, d, 0)

        qd = q * jnp.exp(a_sub)
        kd = k * jnp.exp(b_sub)
        tot1 = a_sub[2 * SUB - 1:2 * SUB, :]
        tot2 = a_sub[3 * SUB - 1:3 * SUB, :]
        q_mid = qd * jnp.where(row >= 3 * SUB, jnp.exp(tot2), 1.0)
        k_mid = kd * jnp.where(row < SUB, jnp.exp(tot1), 1.0)
        a_adj = _dot_nt(qd.astype(BF16), kd.astype(BF16))
        a_mid = _dot_nt(q_mid.astype(BF16), k_mid.astype(BF16))
        a_off = jnp.where(rs == cs + 1, a_adj, jnp.where(rs - cs >= 2, a_mid, 0.0))
        vb = v.astype(BF16)
        out = out + _dot(a_off.astype(BF16), vb)

        st = st_ref[...]
        out = out + _dot_nt((q * jnp.exp(a_full)).astype(BF16), st.astype(BF16))
        k_dec = (k * jnp.exp(b_full)).astype(BF16)
        st_ref[...] = st * jnp.exp(a_full[C - 1:C, :]) + _dot_tn(vb, k_dec)

        ms = jnp.mean(out * out, axis=-1, keepdims=True)
        on = out * lax.rsqrt(ms + RMS_EPS) * gn
        o_ref[sl, :] = (on * _silu(g_ref[sl, :].astype(F32))).astype(BF16)
        return carry

    lax.fori_loop(0, tc // C, chunk, 0, unroll=4)


def _hgrn(hg, lb, gn, *, batch, seq, tc):
    n_tok = batch * seq
    nt = seq // tc
    part = lambda p: pl.BlockSpec((tc, HGRN_DK), lambda b, h, i, p=p: (b * nt + i, p * HGRN_HEADS + h))
    return pl.pallas_call(
        functools.partial(_hgrn_body, tc=tc),
        grid=(batch, HGRN_HEADS, nt),
        in_specs=[part(0), part(1), part(2), part(3),
                  pl.BlockSpec((1, HGRN_DK), lambda b, h, i: (0, h)),
                  pl.BlockSpec((1, HGRN_DK), lambda b, h, i: (0, 0))],
        out_specs=pl.BlockSpec((tc, HGRN_DK), lambda b, h, i: (b * nt + i, h)),
        out_shape=jax.ShapeDtypeStruct((n_tok, HGRN_WIDTH), BF16),
        scratch_shapes=[pltpu.VMEM((HGRN_DK, HGRN_DK), F32)],
        compiler_params=_params(("parallel", "parallel", "arbitrary")),
        name="hgrn2",
    )(hg, hg, hg, hg, lb, gn)


def _outproj_body(on_ref, oh_ref, x_ref, gno_ref, wo1_ref, wo2_ref, gf_ref, wr_ref, br_ref,
                  h_ref, hn_ref, route_ref):
    a = on_ref[...].astype(F32)
    ms = jnp.mean(a * a, axis=-1, keepdims=True)
    an = (a * lax.rsqrt(ms + RMS_EPS) * gno_ref[...]).astype(BF16)
    h = x_ref[...] + _dot(an, wo1_ref[...]) + _dot(oh_ref[...], wo2_ref[...])
    h_ref[...] = h
    ms2 = jnp.mean(h * h, axis=-1, keepdims=True)
    hn = h * lax.rsqrt(ms2 + RMS_EPS) * gf_ref[...]
    hn_ref[...] = hn.astype(BF16)

    hn_hi = hn.astype(BF16)
    hn_lo = (hn - hn_hi.astype(F32)).astype(BF16)
    lg = (_dot(hn_hi, wr_ref[0]) + _dot(hn_lo, wr_ref[0]) + _dot(hn_hi, wr_ref[1])) + br_ref[...]

    lane = lax.broadcasted_iota(I32, (1, LANES), 1).astype(F32)
    first = lambda hit: jnp.min(jnp.where(hit, lane, 1e9), axis=-1, keepdims=True)
    gmask = lane < MOE_GROUPS
    lgm = jnp.where(gmask, lg, -jnp.inf)
    mg = jnp.max(lgm, axis=-1, keepdims=True)
    pg_top = 1.0 / jnp.sum(jnp.where(gmask, jnp.exp(lgm - mg), 0.0), axis=-1, keepdims=True)
    gidx = first(lgm == mg)
    eloc = lane - MOE_GROUPS
    emask = (eloc >= 0) & (eloc < N_EXPERTS) & (jnp.floor(eloc * (1.0 / EXPERTS_PER_GROUP)) == gidx)
    le1 = jnp.where(emask, lg, -jnp.inf)
    m1 = jnp.max(le1, axis=-1, keepdims=True)
    i1 = first(le1 == m1)
    le2 = jnp.where(lane == i1, -jnp.inf, le1)
    m2 = jnp.max(le2, axis=-1, keepdims=True)
    i2 = first(le2 == m2)
    e2 = jnp.exp(m2 - m1)
    w1 = pg_top / (1.0 + e2)
    w2 = pg_top * e2 / (1.0 + e2)
    route_ref[...] = jnp.where(lane == 0, i1 - MOE_GROUPS,
                     jnp.where(lane == 1, i2 - MOE_GROUPS,
                     jnp.where(lane == 2, w1, jnp.where(lane == 3, w2, 0.0))))


def _outproj(o_nsa, o_hgrn, x2, gno, wo1, wo2, gf, wr, br, *, tm):
    n_tok = x2.shape[0]
    row_spec = lambda cols: pl.BlockSpec((tm, cols), lambda i: (i, 0))
    full = lambda a: pl.BlockSpec(a.shape, lambda i: (0,) * a.ndim)
    return pl.pallas_call(
        _outproj_body,
        grid=(n_tok // tm,),
        in_specs=[row_spec(NSA_WIDTH), row_spec(HGRN_WIDTH), row_spec(D_MODEL), full(gno),
                  full(wo1), full(wo2), full(gf), full(wr), full(br)],
        out_specs=(row_spec(D_MODEL), row_spec(D_MODEL), row_spec(LANES)),
        out_shape=(jax.ShapeDtypeStruct((n_tok, D_MODEL), F32),
                   jax.ShapeDtypeStruct((n_tok, D_MODEL), BF16),
                   jax.ShapeDtypeStruct((n_tok, LANES), F32)),
        compiler_params=_params(("parallel",)),
        name="outproj_router",
    )(o_nsa, o_hgrn, x2, gno, wo1, wo2, gf, wr, br)


def _expert_body(be_ref, nb_ref, x_ref, w1_ref, w3_ref, w2_ref, y_ref):
    @pl.when(pl.program_id(0) < nb_ref[0])
    def _():
        x = x_ref[...]
        hid = _silu(_dot(x, w1_ref[0])) * _dot(x, w3_ref[0])
        y_ref[...] = _dot(hid.astype(BF16), w2_ref[0]).astype(BF16)

    @pl.when(pl.program_id(0) >= nb_ref[0])
    def _():
        y_ref[...] = jnp.zeros_like(y_ref)


def _experts(blk_e, n_used, x_pad, w1, w3, w2):
    rows = x_pad.shape[0]
    nblk = rows // MOE_BLOCK
    wspec = lambda a: pl.BlockSpec((1,) + a.shape[1:], lambda i, be, nb: (be[i], 0, 0))
    grid_spec = pltpu.PrefetchScalarGridSpec(
        num_scalar_prefetch=2,
        grid=(nblk,),
        in_specs=[pl.BlockSpec((MOE_BLOCK, D_MODEL), lambda i, be, nb: (i, 0)),
                  wspec(w1), wspec(w3), wspec(w2)],
        out_specs=pl.BlockSpec((MOE_BLOCK, D_MODEL), lambda i, be, nb: (i, 0)),
    )
    return pl.pallas_call(
        _expert_body,
        grid_spec=grid_spec,
        out_shape=jax.ShapeDtypeStruct((rows, D_MODEL), BF16),
        compiler_params=_params(("arbitrary",)),
        name="experts",
    )(blk_e, n_used, x_pad, w1, w3, w2)


def _pair_groups(w):
    rows, cols = w.shape
    z = jnp.zeros((NSA_GROUPS, rows, NSA_GROUPS, cols), w.dtype)
    for g in range(NSA_GROUPS):
        z = z.at[g, :, g, :].set(w)
    return z


def _compress_weights(w1, w2, pos):
    w1r = w1.reshape(2, CMP_STRIDE, HEAD_DIM, CMP_HIDDEN)
    halves = []
    for half in range(2):
        z = jnp.zeros((CMP_STRIDE, NSA_GROUPS, HEAD_DIM, NSA_GROUPS, CMP_HIDDEN), F32)
        for g in range(NSA_GROUPS):
            z = z.at[:, g, :, g, :].set(w1r[half])
        halves.append(z.reshape(CMP_STRIDE * KV_DIM, NSA_GROUPS * CMP_HIDDEN))
    w1p = jnp.stack(halves).astype(BF16)
    w2p = _pair_groups(w2).reshape(NSA_GROUPS * CMP_HIDDEN, KV_DIM).astype(BF16)
    posr = pos.reshape(2, CMP_STRIDE, 1, HEAD_DIM)
    posp = jnp.broadcast_to(posr, (2, CMP_STRIDE, NSA_GROUPS, HEAD_DIM)).reshape(2, CMP_STRIDE * KV_DIM)
    return w1p, w2p, posp


def kernel(x, attn_norm_g, ffn_norm_g, w_in, w_out, nsa_q_norm_g, nsa_k_norm_g, cmp_pos_k, cmp_pos_v,
           cmp_wk1, cmp_wk2, cmp_wv1, cmp_wv2, nsa_out_norm_g, hgrn_lb_logits, hgrn_out_norm_g,
           moe_w_group, moe_b_group, moe_w_expert, moe_b_expert, moe_w1, moe_w3, moe_w2):
    batch, seq, _ = x.shape
    n_tok = batch * seq
    assert seq % SEL_CHUNK == 0 and seq >= WINDOW + 128 and seq // SEL_BLOCK <= HEAD_DIM
    depth = w_in.shape[0]
    lb_all = jnp.cumsum(jax.nn.softmax(hgrn_lb_logits.astype(F32), axis=0), axis=0)
    tm = min(512, n_tok)
    h = x.reshape(n_tok, D_MODEL)
    for layer in range(depth):
        wl = w_in[layer]
        gate_cols = wl[:, QKV_COLS:QKV_COLS + NSA_HEADS * NSA_BRANCHES]
        w_in_p = jnp.concatenate(
            [wl[:, :QKV_COLS], wl[:, QKV_COLS + NSA_HEADS * NSA_BRANCHES:],
             jnp.pad(gate_cols, ((0, 0), (0, LANES - NSA_HEADS * NSA_BRANCHES)))], axis=1).astype(BF16)
        gq128 = jnp.tile(nsa_q_norm_g[layer], 2)[None, :]
        gk128 = jnp.tile(nsa_k_norm_g[layer], (1, 2))
        g_attn = attn_norm_g[layer][None, :]

        qt, ksel, vselt, kwin, vwint, kc_raw, vc_raw, gates_t, hg = _inproj(
            h, g_attn, w_in_p, gq128, gk128, seq=seq, tm=tm)

        wk1p, wk2p, poskp = _compress_weights(cmp_wk1[layer], cmp_wk2[layer], cmp_pos_k[layer])
        wv1p, wv2p, posvp = _compress_weights(cmp_wv1[layer], cmp_wv2[layer], cmp_pos_v[layer])
        nrow = seq // CMP_STRIDE
        kcmp, vcmpt = _compress(kc_raw.reshape(batch, nrow, CMP_STRIDE * KV_DIM),
                                vc_raw.reshape(batch, nrow, CMP_STRIDE * KV_DIM),
                                poskp, posvp, wk1p, wv1p, wk2p, wv2p, gk128)

        o_nsa = _attention(qt, kcmp, vcmpt, ksel, vselt, kwin, vwint, gates_t,
                           batch=batch, seq=seq, tq=256)
        o_hgrn = _hgrn(hg, lb_all[layer][None, :], hgrn_out_norm_g[layer][None, :],
                       batch=batch, seq=seq, tc=min(512, seq))

        wo = w_out[layer].astype(BF16)
        wr = jnp.concatenate([moe_w_group[layer], moe_w_expert[layer]], axis=1)
        wr = jnp.pad(wr, ((0, 0), (0, LANES - wr.shape[1])))
        wr_hi = wr.astype(BF16)
        wr_lo = (wr - wr_hi.astype(F32)).astype(BF16)
        br = jnp.pad(jnp.concatenate([moe_b_group[layer], moe_b_expert[layer]]),
                     (0, LANES - MOE_GROUPS - N_EXPERTS))[None, :]
        h, hn, route = _outproj(o_nsa, o_hgrn, h, nsa_out_norm_g[layer][None, :],
                                wo[:NSA_WIDTH], wo[NSA_WIDTH:], ffn_norm_g[layer][None, :],
                                jnp.stack([wr_hi, wr_lo]), br, tm=tm)

        e_flat = route[:, 0:2].astype(I32).reshape(-1)
        w_flat = route[:, 2:4].reshape(-1)
        n_asg = e_flat.shape[0]
        onehot = (e_flat[:, None] == jnp.arange(N_EXPERTS)[None, :]).astype(I32)
        rank = jnp.take_along_axis(jnp.cumsum(onehot, axis=0) - onehot, e_flat[:, None], axis=1)[:, 0]
        counts = jnp.sum(onehot, axis=0)
        padded = (counts + MOE_BLOCK - 1) // MOE_BLOCK * MOE_BLOCK
        pend = jnp.cumsum(padded)
        dest = (pend - padded)[e_flat] + rank
        p_rows = n_asg + N_EXPERTS * MOE_BLOCK
        nblk = p_rows // MOE_BLOCK
        src_tok = jnp.zeros((p_rows,), I32).at[dest].set(jnp.arange(n_asg, dtype=I32) // 2)
        x_pad = hn[src_tok]
        blk_e = jnp.minimum(jnp.sum(jnp.arange(nblk)[:, None] * MOE_BLOCK >= pend[None, :], axis=1),
                            N_EXPERTS - 1).astype(I32)
        n_used = (pend[-1] // MOE_BLOCK).astype(I32)[None]
        y_pad = _experts(blk_e, n_used, x_pad, moe_w1[layer].astype(BF16), moe_w3[layer].astype(BF16),
                         moe_w2[layer].astype(BF16))
        y = (y_pad[dest].astype(F32) * w_flat[:, None]).reshape(n_tok, 2, D_MODEL)
        h = h + y[:, 0] + y[:, 1]
    return h.reshape(batch, seq, D_MODEL)
```

```python
import functools

import jax
import jax.numpy as jnp
import numpy as np
from jax import lax
from jax.experimental import pallas as pl
from jax.experimental.pallas import tpu as pltpu

F32 = jnp.float32
BF16 = jnp.bfloat16
I32 = jnp.int32

D_MODEL = 1024
NSA_HEADS = 8
NSA_GROUPS = 2
NSA_REP = NSA_HEADS // NSA_GROUPS
HEAD_DIM = 64
NSA_WIDTH = NSA_HEADS * HEAD_DIM
KV_DIM = NSA_GROUPS * HEAD_DIM
NSA_BRANCHES = 3
CMP_STRIDE = 16
CMP_LEN = 32
CMP_HIDDEN = 256
SEL_BLOCK = 64
SEL_TOPN = 16
WINDOW = 512
HGRN_WIDTH = D_MODEL - NSA_WIDTH
HGRN_HEADS = 4
HGRN_DK = 128
HGRN_CHUNK = 64
HGRN_SUB = 16
MOE_GROUPS = 4
EXPERTS_PER_GROUP = 8
N_EXPERTS = MOE_GROUPS * EXPERTS_PER_GROUP
EXPERT_FF = 512
MOE_BLOCK = 256
RMS_EPS = 1e-6
NEG_INF = -1e30

LANES = 128
QKV_COLS = NSA_WIDTH + 6 * KV_DIM
HG_COLS = 4 * HGRN_WIDTH
IN_COLS_PAD = QKV_COLS + HG_COLS + LANES
GATE_ROWS = 32
VT_BLOCK = 128
LOG2E = 1.4426950408889634
VMEM_LIMIT = 56 * 1024 * 1024


def _dot(a, b):
    return jnp.dot(a, b, preferred_element_type=F32)


def _dot_nt(a, b):
    return lax.dot_general(a, b, (((1,), (1,)), ((), ())), preferred_element_type=F32)


def _dot_tn(a, b):
    return lax.dot_general(a, b, (((0,), (0,)), ((), ())), preferred_element_type=F32)


def _split3(x):
    a = x.astype(BF16)
    r = x - a.astype(F32)
    b = r.astype(BF16)
    c = (r - b.astype(F32)).astype(BF16)
    return a, b, c


def _silu(x):
    return x * jax.nn.sigmoid(x)


def _params(sem):
    return pltpu.CompilerParams(dimension_semantics=sem, vmem_limit_bytes=VMEM_LIMIT)


def _seg_rms(blk, low, gain):
    sq = blk * blk
    s_lo = jnp.sum(jnp.where(low, sq, 0.0), axis=-1, keepdims=True)
    s_hi = jnp.sum(jnp.where(low, 0.0, sq), axis=-1, keepdims=True)
    inv = jnp.where(low, lax.rsqrt(s_lo * (1.0 / HEAD_DIM) + RMS_EPS),
                    lax.rsqrt(s_hi * (1.0 / HEAD_DIM) + RMS_EPS))
    return blk * inv * gain


def _inproj_body(x_ref, g_ref, w_ref, gq_ref, gk_ref,
                 qt_ref, ksel_ref, vselt_ref, kwin_ref, vwint_ref, kc_ref, vc_ref,
                 gates_ref, hg_ref, *, tm, seq):
    x = x_ref[...]
    ms = jnp.mean(x * x, axis=-1, keepdims=True)
    n = (x * lax.rsqrt(ms + RMS_EPS) * g_ref[...]).astype(BF16)
    y = _dot(n, w_ref[...])

    lane = lax.broadcasted_iota(I32, (1, LANES), 1)
    low = lane < HEAD_DIM
    gq = gq_ref[...]
    scale = HEAD_DIM ** -0.5 * LOG2E
    for j in range(NSA_HEADS // 2):
        nrm_t = (_seg_rms(y[:, LANES * j:LANES * (j + 1)], low, gq) * scale).T.astype(BF16)
        qt_ref[2 * j] = nrm_t[0:HEAD_DIM]
        qt_ref[2 * j + 1] = nrm_t[HEAD_DIM:]

    def kv_block(idx):
        c0 = NSA_WIDTH + LANES * idx
        return y[:, c0:c0 + LANES]

    def split_groups(blk, hi_fill):
        g0 = jnp.where(low, blk, hi_fill)
        g1 = jnp.where(low, pltpu.roll(blk, HEAD_DIM, 1), hi_fill)
        return g0.astype(BF16), g1.astype(BF16)

    def store_transposed(vt_ref, blk):
        blk_t = blk.T.astype(BF16)
        for g in range(NSA_GROUPS):
            for c in range(tm // VT_BLOCK):
                vt_ref[g, c] = blk_t[g * HEAD_DIM:(g + 1) * HEAD_DIM, c * VT_BLOCK:(c + 1) * VT_BLOCK]

    kc_ref[...] = kv_block(0).astype(BF16)
    vc_ref[...] = kv_block(1).astype(BF16)

    row = lax.broadcasted_iota(I32, (tm, LANES), 0)
    t = (pl.program_id(0) * tm) % seq + row
    onehot = jnp.where(lane - HEAD_DIM == lax.shift_right_logical(t, 6), 1.0, 0.0)
    ks = _seg_rms(kv_block(2), low, gk_ref[1:2, :])
    ksel_ref[0], ksel_ref[1] = split_groups(ks, onehot)
    store_transposed(vselt_ref, kv_block(3))
    kw = _seg_rms(kv_block(4), low, gk_ref[2:3, :])
    kwin_ref[0], kwin_ref[1] = split_groups(kw, 0.0)
    store_transposed(vwint_ref, kv_block(5))

    hg_ref[...] = y[:, QKV_COLS:QKV_COLS + HG_COLS].astype(BF16)
    gates_ref[...] = y[:, QKV_COLS + HG_COLS:].T[0:GATE_ROWS]


def _inproj(x2, g_attn, w_in_p, gq128, gk128, *, seq, tm):
    n_tok = x2.shape[0]
    grid = (n_tok // tm,)
    row_spec = lambda cols: pl.BlockSpec((tm, cols), lambda i: (i, 0))
    k_spec = pl.BlockSpec((NSA_GROUPS, tm, LANES), lambda i: (0, i, 0))
    vt_spec = pl.BlockSpec((NSA_GROUPS, tm // VT_BLOCK, HEAD_DIM, VT_BLOCK), lambda i: (0, i, 0, 0))
    vt_shape = jax.ShapeDtypeStruct((NSA_GROUPS, n_tok // VT_BLOCK, HEAD_DIM, VT_BLOCK), BF16)
    full = lambda a: pl.BlockSpec(a.shape, lambda i: (0,) * a.ndim)
    out_shape = (
        jax.ShapeDtypeStruct((NSA_HEADS, HEAD_DIM, n_tok), BF16),
        jax.ShapeDtypeStruct((NSA_GROUPS, n_tok, LANES), BF16),
        vt_shape,
        jax.ShapeDtypeStruct((NSA_GROUPS, n_tok, LANES), BF16),
        vt_shape,
        jax.ShapeDtypeStruct((n_tok, LANES), BF16),
        jax.ShapeDtypeStruct((n_tok, LANES), BF16),
        jax.ShapeDtypeStruct((GATE_ROWS, n_tok), F32),
        jax.ShapeDtypeStruct((n_tok, HG_COLS), BF16),
    )
    out_specs = (pl.BlockSpec((NSA_HEADS, HEAD_DIM, tm), lambda i: (0, 0, i)),
                 k_spec, vt_spec, k_spec, vt_spec, row_spec(LANES), row_spec(LANES),
                 pl.BlockSpec((GATE_ROWS, tm), lambda i: (0, i)), row_spec(HG_COLS))
    return pl.pallas_call(
        functools.partial(_inproj_body, tm=tm, seq=seq),
        grid=grid,
        in_specs=[row_spec(D_MODEL), full(g_attn), full(w_in_p), full(gq128), full(gk128)],
        out_specs=out_specs,
        out_shape=out_shape,
        compiler_params=_params(("parallel",)),
        name="inproj",
    )(x2, g_attn, w_in_p, gq128, gk128)


def _compress_body(kc_ref, vc_ref, posk_ref, posv_ref, wk1_ref, wv1_ref, wk2_ref, wv2_ref,
                   gk_ref, kco_ref, vco_ref, *, nrow):
    lane = lax.broadcasted_iota(I32, (1, LANES), 1)
    low = lane < HEAD_DIM

    def mlp(x_ref, pos_ref, w1_ref, w2_ref):
        x = x_ref[0].astype(F32)
        ha = _dot((x + pos_ref[0:1, :]).astype(BF16), w1_ref[0])
        hb = _dot((x + pos_ref[1:2, :]).astype(BF16), w1_ref[1])
        hid = _silu(ha + pltpu.roll(hb, nrow - 1, 0))
        return _dot(hid.astype(BF16), w2_ref[...])

    def split_groups(blk):
        g0 = jnp.where(low, blk, 0.0)
        g1 = jnp.where(low, pltpu.roll(blk, HEAD_DIM, 1), 0.0)
        return g0.astype(BF16), g1.astype(BF16)

    kc = _seg_rms(mlp(kc_ref, posk_ref, wk1_ref, wk2_ref), low, gk_ref[0:1, :])
    kco_ref[0, 0], kco_ref[0, 1] = split_groups(kc)
    vc_t = mlp(vc_ref, posv_ref, wv1_ref, wv2_ref).T.astype(BF16)
    vco_ref[0, 0] = vc_t[0:HEAD_DIM]
    vco_ref[0, 1] = vc_t[HEAD_DIM:]


def _compress(kc3, vc3, posk, posv, wk1, wv1, wk2, wv2, gk128):
    batch, nrow, width = kc3.shape
    full = lambda a: pl.BlockSpec(a.shape, lambda b: (0,) * a.ndim)
    in_spec = pl.BlockSpec((1, nrow, width), lambda b: (b, 0, 0))
    k_spec = pl.BlockSpec((1, NSA_GROUPS, nrow, LANES), lambda b: (b, 0, 0, 0))
    vt_spec = pl.BlockSpec((1, NSA_GROUPS, HEAD_DIM, nrow), lambda b: (b, 0, 0, 0))
    return pl.pallas_call(
        functools.partial(_compress_body, nrow=nrow),
        grid=(batch,),
        in_specs=[in_spec, in_spec, full(posk), full(posv), full(wk1), full(wv1),
                  full(wk2), full(wv2), full(gk128)],
        out_specs=(k_spec, vt_spec),
        out_shape=(jax.ShapeDtypeStruct((batch, NSA_GROUPS, nrow, LANES), BF16),
                   jax.ShapeDtypeStruct((batch, NSA_GROUPS, HEAD_DIM, nrow), BF16)),
        compiler_params=_params(("parallel",)),
        name="compress",
    )(kc3, vc3, posk, posv, wk1, wv1, wk2, wv2, gk128)


SEL_CHUNK = 512


def _vt_cols(vt_ref, start, n):
    b0 = start // VT_BLOCK
    return jnp.concatenate([vt_ref[0, b0 + i] for i in range(n // VT_BLOCK)], axis=1)


def _attn_body(qt_ref, kc_ref, vct_ref, ks_ref, vst_ref, kw_ref, vwt_ref, gt_ref, o_ref,
               *, tq, seq, ncmp):
    g = pl.program_id(1)
    t0 = pl.program_id(2) * tq
    cols = NSA_REP * tq
    qt = jnp.concatenate([qt_ref[r] for r in range(NSA_REP)], axis=1)
    q_pad = jnp.concatenate([qt, jnp.zeros((HEAD_DIM, cols), BF16)], axis=0)
    tcol = t0 + (lax.broadcasted_iota(I32, (1, cols), 1) & (tq - 1))
    tlane = t0 + lax.broadcasted_iota(I32, (1, tq), 1)
    per_head = lambda a: jnp.concatenate([a] * NSA_REP, axis=1)

    s = _dot(kc_ref[0, 0], q_pad)
    cpos = lax.broadcasted_iota(I32, (ncmp, 1), 0) * CMP_STRIDE + (CMP_LEN - 1)
    cmask = cpos <= tcol
    s = jnp.where(cmask, s, NEG_INF)
    e = jnp.where(cmask, jnp.exp2(s - jnp.max(s, axis=0, keepdims=True)), 0.0)
    den = jnp.sum(e, axis=0, keepdims=True)
    p_c = e / jnp.where(den > 0.0, den, 1.0)
    o_c = _dot(vct_ref[0, 0], p_c.astype(BF16))

    p_sum = p_c[:, 0:tq]
    for r in range(1, NSA_REP):
        p_sum = p_sum + p_c[:, r * tq:(r + 1) * tq]
    nblk = seq // SEL_BLOCK
    jj = lax.broadcasted_iota(I32, (nblk, ncmp), 0) * SEL_BLOCK
    cc = lax.broadcasted_iota(I32, (nblk, ncmp), 1) * CMP_STRIDE
    ov = jnp.maximum(jnp.minimum(cc + CMP_LEN, jj + SEL_BLOCK) - jnp.maximum(cc, jj), 0)
    ov = (ov.astype(F32) * (1.0 / CMP_LEN)).astype(BF16)
    p1, p2, p3 = _split3(p_sum)
    imp = _dot(ov, p1) + _dot(ov, p2) + _dot(ov, p3)

    jb = lax.broadcasted_iota(I32, (nblk, tq), 0)
    cur = lax.shift_right_logical(t0 + lax.broadcasted_iota(I32, (nblk, tq), 1), 6)
    valid = jb <= cur
    forced = (jb == cur) | (jb == 0)
    imp = jnp.where(forced, jnp.inf, jnp.where(valid, imp, -jnp.inf))
    rank = jnp.zeros((nblk, tq), F32)
    for j2 in range(nblk):
        other = imp[j2:j2 + 1, :]
        ahead = (other > imp) | ((other == imp) & (jb > j2))
        rank = rank + jnp.where(ahead, 1.0, 0.0)
    sel = (rank < float(min(SEL_TOPN, nblk))) & valid
    bias = jnp.where(sel, 0.0, NEG_INF).astype(BF16)
    if nblk < HEAD_DIM:
        bias = jnp.concatenate([bias, jnp.zeros((HEAD_DIM - nblk, tq), BF16)], axis=0)
    q_sel = jnp.concatenate([qt, per_head(bias)], axis=0)

    def sel_step(start, extra, carry):
        m_old, l_old, acc = carry
        sc = _dot(ks_ref[0, pl.ds(start, SEL_CHUNK), :], q_sel)
        if extra is not None:
            sc = sc + extra
        m_new = jnp.maximum(m_old, jnp.max(sc, axis=0, keepdims=True))
        alpha = jnp.exp2(m_old - m_new)
        p = jnp.exp2(sc - m_new)
        l_new = alpha * l_old + jnp.sum(p, axis=0, keepdims=True)
        acc = alpha * acc + _dot(_vt_cols(vst_ref, start, SEL_CHUNK), p.astype(BF16))
        return m_new, l_new, acc

    n_full = t0 // SEL_CHUNK
    init = (jnp.full((1, cols), -3.0e38, F32), jnp.zeros((1, cols), F32), jnp.zeros((HEAD_DIM, cols), F32))
    carry = lax.fori_loop(
        0, n_full, lambda ci, c: sel_step(pl.multiple_of(ci * SEL_CHUNK, SEL_CHUNK), None, c), init)
    tail0 = pl.multiple_of(n_full * SEL_CHUNK, SEL_CHUNK)
    kpos = tail0 + lax.broadcasted_iota(I32, (SEL_CHUNK, 1), 0)
    causal = jnp.where(kpos <= tlane, 0.0, NEG_INF)
    _, l_s, acc_s = sel_step(tail0, per_head(causal), carry)
    o_s = acc_s / l_s

    band = WINDOW + tq
    w0 = pl.multiple_of(jnp.maximum(t0 - WINDOW, 0), tq)
    sw = _dot(kw_ref[0, pl.ds(w0, band), :], q_pad)
    dist = tlane - (w0 + lax.broadcasted_iota(I32, (band, 1), 0))
    in_win = lax.bitcast_convert_type(dist, jnp.uint32) < jnp.uint32(WINDOW)
    sw = sw + per_head(jnp.where(in_win, 0.0, NEG_INF))
    pw = jnp.exp2(sw - jnp.max(sw, axis=0, keepdims=True))
    o_w = _dot(_vt_cols(vwt_ref, w0, band), pw.astype(BF16)) / jnp.sum(pw, axis=0, keepdims=True)

    def gate(r, br):
        col = (g * NSA_REP + r) * NSA_BRANCHES + br
        return jax.nn.sigmoid(gt_ref[pl.ds(col, 1), :])

    heads = []
    for r in range(NSA_REP):
        sl = slice(r * tq, (r + 1) * tq)
        heads.append(gate(r, 0) * o_c[:, sl] + gate(r, 1) * o_s[:, sl] + gate(r, 2) * o_w[:, sl])
    o_ref[...] = jnp.concatenate(heads, axis=0).T.astype(BF16)


def _attention(qt, kcmp, vcmpt, ksel, vselt, kwin, vwint, gates_t, *, batch, seq, tq):
    n_tok = batch * seq
    ncmp = kcmp.shape[2]
    nq = seq // tq
    nvb = seq // VT_BLOCK
    k_spec = pl.BlockSpec((1, seq, LANES), lambda b, g, i: (g, b, 0))
    vt_spec = pl.BlockSpec((1, nvb, HEAD_DIM, VT_BLOCK), lambda b, g, i: (g, b, 0, 0))
    return pl.pallas_call(
        functools.partial(_attn_body, tq=tq, seq=seq, ncmp=ncmp),
        grid=(batch, NSA_GROUPS, nq),
        in_specs=[
            pl.BlockSpec((NSA_REP, HEAD_DIM, tq), lambda b, g, i: (g, 0, b * nq + i)),
            pl.BlockSpec((1, 1, ncmp, LANES), lambda b, g, i: (b, g, 0, 0)),
            pl.BlockSpec((1, 1, HEAD_DIM, ncmp), lambda b, g, i: (b, g, 0, 0)),
            k_spec, vt_spec, k_spec, vt_spec,
            pl.BlockSpec((GATE_ROWS, tq), lambda b, g, i: (0, b * nq + i)),
        ],
        out_specs=pl.BlockSpec((tq, NSA_REP * HEAD_DIM), lambda b, g, i: (b * nq + i, g)),
        out_shape=jax.ShapeDtypeStruct((n_tok, NSA_WIDTH), BF16),
        compiler_params=_params(("parallel", "parallel", "arbitrary")),
        name="nsa_attention",
    )(qt, kcmp, vcmpt, ksel, vselt, kwin, vwint, gates_t)


def _hgrn_body(q_ref, f_ref, i_ref, g_ref, lb_ref, gn_ref, o_ref, st_ref, *, tc):
    C, SUB = HGRN_CHUNK, HGRN_SUB

    @pl.when(pl.program_id(2) == 0)
    def _():
        st_ref[...] = jnp.zeros_like(st_ref)

    r = lax.broadcasted_iota(I32, (C, C), 0)
    c = lax.broadcasted_iota(I32, (C, C), 1)
    rs, cs = r >> 4, c >> 4
    one = lambda m: jnp.where(m, 1.0, 0.0).astype(BF16)
    sum_mats = jnp.concatenate([one(c <= r), one((c <= r) & (rs == cs)),
                                one((c > r) & (rs == cs)), one(c > r)], axis=0)
    row = lax.broadcasted_iota(I32, (C, 1), 0)
    lb = lb_ref[...]
    gn = gn_ref[...]

    def chunk(ci, carry):
        sl = pl.ds(pl.multiple_of(ci * C, C), C)
        q = _silu(q_ref[sl, :].astype(F32))
        fg = lb + (1.0 - lb) * jax.nn.sigmoid(f_ref[sl, :].astype(F32))
        k = 1.0 - fg
        v = i_ref[sl, :].astype(F32)
        l1, l2, l3 = _split3(jnp.log(fg))
        sums = _dot(sum_mats, l1) + _dot(sum_mats, l2) + _dot(sum_mats, l3)
        a_full = sums[0:C]
        a_sub = sums[C:2 * C]
        b_sub = sums[2 * C:3 * C]
        b_full = sums[3 * C:4 * C]

        out = jnp.sum(q * k, axis=-1, keepdims=True) * v
        for d in range(1, SUB):
            ok = (row & (SUB - 1)) >= d
            arg = jnp.where(ok, a_sub - pltpu.roll(a_sub, d, 0), NEG_INF)
            w = jnp.sum(q * pltpu.roll(k, d, 0) * jnp.exp(arg), axis=-1, keepdims=True)
            out = out + w * pltpu.roll(v, d, 0)

        qd = q * jnp.exp(a_sub)
        kd = k * jnp.exp(b_sub)
        tot1 = a_sub[2 * SUB - 1:2 * SUB, :]
        tot2 = a_sub[3 * SUB - 1:3 * SUB, :]
        q_mid = qd * jnp.where(row >= 3 * SUB, jnp.exp(tot2), 1.0)
        k_mid = kd * jnp.where(row < SUB, jnp.exp(tot1), 1.0)
        a_adj = _dot_nt(qd.astype(BF16), kd.astype(BF16))
        a_mid = _dot_nt(q_mid.astype(BF16), k_mid.astype(BF16))
        a_off = jnp.where(rs == cs + 1, a_adj, jnp.where(rs - cs >= 2, a_mid, 0.0))
        vb = v.astype(BF16)
        out = out + _dot(a_off.astype(BF16), vb)

        st = st_ref[...]
        out = out + _dot_nt((q * jnp.exp(a_full)).astype(BF16), st.astype(BF16))
        k_dec = (k * jnp.exp(b_full)).astype(BF16)
        st_ref[...] = st * jnp.exp(a_full[C - 1:C, :]) + _dot_tn(vb, k_dec)

        ms = jnp.mean(out * out, axis=-1, keepdims=True)
        on = out * lax.rsqrt(ms + RMS_EPS) * gn
        o_ref[sl, :] = (on * _silu(g_ref[sl, :].astype(F32))).astype(BF16)
        return carry

    lax.fori_loop(0, tc // C, chunk, 0, unroll=4)


def _hgrn(hg, lb, gn, *, batch, seq, tc):
    n_tok = batch * seq
    nt = seq // tc
    part = lambda p: pl.BlockSpec((tc, HGRN_DK), lambda b, h, i, p=p: (b * nt + i, p * HGRN_HEADS + h))
    return pl.pallas_call(
        functools.partial(_hgrn_body, tc=tc),
        grid=(batch, HGRN_HEADS, nt),
        in_specs=[part(0), part(1), part(2), part(3),
                  pl.BlockSpec((1, HGRN_DK), lambda b, h, i: (0, h)),
                  pl.BlockSpec((1, HGRN_DK), lambda b, h, i: (0, 0))],
        out_specs=pl.BlockSpec((tc, HGRN_DK), lambda b, h, i: (b * nt + i, h)),
        out_shape=jax.ShapeDtypeStruct((n_tok, HGRN_WIDTH), BF16),
        scratch_shapes=[pltpu.VMEM((HGRN_DK, HGRN_DK), F32)],
        compiler_params=_params(("parallel", "parallel", "arbitrary")),
        name="hgrn2",
    )(hg, hg, hg, hg, lb, gn)


def _outproj_body(on_ref, oh_ref, x_ref, gno_ref, wo1_ref, wo2_ref, gf_ref, wr_ref, br_ref,
                  h_ref, hn_ref, route_ref):
    a = on_ref[...].astype(F32)
    ms = jnp.mean(a * a, axis=-1, keepdims=True)
    an = (a * lax.rsqrt(ms + RMS_EPS) * gno_ref[...]).astype(BF16)
    h = x_ref[...] + _dot(an, wo1_ref[...]) + _dot(oh_ref[...], wo2_ref[...])
    h_ref[...] = h
    ms2 = jnp.mean(h * h, axis=-1, keepdims=True)
    hn = h * lax.rsqrt(ms2 + RMS_EPS) * gf_ref[...]
    hn_ref[...] = hn

    hn_hi = hn.astype(BF16)
    hn_lo = (hn - hn_hi.astype(F32)).astype(BF16)
    lg = (_dot(hn_hi, wr_ref[0]) + _dot(hn_lo, wr_ref[0]) + _dot(hn_hi, wr_ref[1])) + br_ref[...]

    lane = lax.broadcasted_iota(I32, (1, LANES), 1).astype(F32)
    first = lambda hit: jnp.min(jnp.where(hit, lane, 1e9), axis=-1, keepdims=True)
    gmask = lane < MOE_GROUPS
    lgm = jnp.where(gmask, lg, -jnp.inf)
    mg = jnp.max(lgm, axis=-1, keepdims=True)
    pg_top = 1.0 / jnp.sum(jnp.where(gmask, jnp.exp(lgm - mg), 0.0), axis=-1, keepdims=True)
    gidx = first(lgm == mg)
    eloc = lane - MOE_GROUPS
    emask = (eloc >= 0) & (eloc < N_EXPERTS) & (jnp.floor(eloc * (1.0 / EXPERTS_PER_GROUP)) == gidx)
    le1 = jnp.where(emask, lg, -jnp.inf)
    m1 = jnp.max(le1, axis=-1, keepdims=True)
    i1 = first(le1 == m1)
    le2 = jnp.where(lane == i1, -jnp.inf, le1)
    m2 = jnp.max(le2, axis=-1, keepdims=True)
    i2 = first(le2 == m2)
    e2 = jnp.exp(m2 - m1)
    w1 = pg_top / (1.0 + e2)
    w2 = pg_top * e2 / (1.0 + e2)
    route_ref[...] = jnp.where(lane == 0, i1 - MOE_GROUPS,
                     jnp.where(lane == 1, i2 - MOE_GROUPS,
                     jnp.where(lane == 2, w1, jnp.where(lane == 3, w2, 0.0))))


def _outproj(o_nsa, o_hgrn, x2, gno, wo1, wo2, gf, wr, br, *, tm):
    n_tok = x2.shape[0]
    row_spec = lambda cols: pl.BlockSpec((tm, cols), lambda i: (i, 0))
    full = lambda a: pl.BlockSpec(a.shape, lambda i: (0,) * a.ndim)
    return pl.pallas_call(
        _outproj_body,
        grid=(n_tok // tm,),
        in_specs=[row_spec(NSA_WIDTH), row_spec(HGRN_WIDTH), row_spec(D_MODEL), full(gno),
                  full(wo1), full(wo2), full(gf), full(wr), full(br)],
        out_specs=(row_spec(D_MODEL), row_spec(D_MODEL), row_spec(LANES)),
        out_shape=(jax.ShapeDtypeStruct((n_tok, D_MODEL), F32),
                   jax.ShapeDtypeStruct((n_tok, D_MODEL), F32),
                   jax.ShapeDtypeStruct((n_tok, LANES), F32)),
        compiler_params=_params(("parallel",)),
        name="outproj_router",
    )(o_nsa, o_hgrn, x2, gno, wo1, wo2, gf, wr, br)


def _expert_body(be_ref, nb_ref, src_ref, hn_ref, w1_ref, w3_ref, w2_ref, y2_ref,
                 xbuf, ybuf, gsem, ssem, *, n_asg):
    i = pl.program_id(0)
    n_used = nb_ref[0]
    R = MOE_BLOCK

    def gather_start(blk, slot):
        def issue(r, c):
            tok = lax.shift_right_logical(jnp.maximum(src_ref[blk * R + r], 0), 1)
            pltpu.make_async_copy(hn_ref.at[pl.ds(tok, 1), :], xbuf.at[slot, pl.ds(r, 1), :],
                                  gsem.at[slot]).start()
            return c
        lax.fori_loop(0, R, issue, 0, unroll=8)

    def gather_wait(slot):
        pltpu.make_async_copy(hn_ref.at[pl.ds(0, R), :], xbuf.at[slot], gsem.at[slot]).wait()

    def scatter_start(blk):
        def issue(r, c):
            a = src_ref[blk * R + r]
            row = jnp.where(a >= 0, (a & 1) * (n_asg // 2) + lax.shift_right_logical(a, 1), n_asg + r)
            pltpu.make_async_copy(ybuf.at[pl.ds(r, 1), :], y2_ref.at[pl.ds(row, 1), :], ssem).start()
            return c
        lax.fori_loop(0, R, issue, 0, unroll=8)

    def scatter_wait():
        pltpu.make_async_copy(ybuf, y2_ref.at[pl.ds(0, R), :], ssem).wait()

    @pl.when(i == 0)
    def _():
        ybuf[...] = jnp.zeros_like(ybuf)
        fill = pltpu.make_async_copy(ybuf, y2_ref.at[pl.ds(n_asg, R), :], ssem)
        fill.start()
        fill.wait()
        gather_start(0, 0)

    @pl.when(i < n_used)
    def _():
        slot = i & 1
        gather_wait(slot)

        @pl.when(i + 1 < n_used)
        def _():
            gather_start(i + 1, 1 - slot)

        x = xbuf[slot].astype(BF16)
        hid = _silu(_dot(x, w1_ref[0])) * _dot(x, w3_ref[0])
        y = _dot(hid.astype(BF16), w2_ref[0])

        @pl.when(i > 0)
        def _():
            scatter_wait()

        ybuf[...] = y
        scatter_start(i)

        @pl.when(i == n_used - 1)
        def _():
            scatter_wait()


def _experts(blk_e, n_used, src_asg, hn, w1, w3, w2):
    p_rows = src_asg.shape[0]
    n_asg = 2 * hn.shape[0]
    nblk = p_rows // MOE_BLOCK
    wspec = lambda a: pl.BlockSpec((1,) + a.shape[1:], lambda i, be, nb, src: (be[i], 0, 0))
    grid_spec = pltpu.PrefetchScalarGridSpec(
        num_scalar_prefetch=3,
        grid=(nblk,),
        in_specs=[pl.BlockSpec(memory_space=pl.ANY), wspec(w1), wspec(w3), wspec(w2)],
        out_specs=pl.BlockSpec(memory_space=pl.ANY),
        scratch_shapes=[pltpu.VMEM((2, MOE_BLOCK, D_MODEL), F32), pltpu.VMEM((MOE_BLOCK, D_MODEL), F32),
                        pltpu.SemaphoreType.DMA((2,)), pltpu.SemaphoreType.DMA],
    )
    return pl.pallas_call(
        functools.partial(_expert_body, n_asg=n_asg),
        grid_spec=grid_spec,
        out_shape=jax.ShapeDtypeStruct((n_asg + MOE_BLOCK, D_MODEL), F32),
        compiler_params=_params(("arbitrary",)),
        name="experts",
    )(blk_e, n_used, src_asg, hn, w1, w3, w2)


def _combine_body(h_ref, y0_ref, y1_ref, rt_ref, o_ref):
    rt = rt_ref[...]
    o_ref[...] = h_ref[...] + (rt[:, 2:3] * y0_ref[...] + rt[:, 3:4] * y1_ref[...])


def _combine(h, y2, route, *, tm):
    n_tok = h.shape[0]
    nb = n_tok // tm
    return pl.pallas_call(
        _combine_body,
        grid=(nb,),
        in_specs=[pl.BlockSpec((tm, D_MODEL), lambda i: (i, 0)),
                  pl.BlockSpec((tm, D_MODEL), lambda i: (i, 0)),
                  pl.BlockSpec((tm, D_MODEL), lambda i: (nb + i, 0)),
                  pl.BlockSpec((tm, LANES), lambda i: (i, 0))],
        out_specs=pl.BlockSpec((tm, D_MODEL), lambda i: (i, 0)),
        out_shape=jax.ShapeDtypeStruct((n_tok, D_MODEL), F32),
        compiler_params=_params(("parallel",)),
        name="moe_combine",
    )(h, y2, y2, route)


def _pair_groups(w):
    rows, cols = w.shape
    z = jnp.zeros((NSA_GROUPS, rows, NSA_GROUPS, cols), w.dtype)
    for g in range(NSA_GROUPS):
        z = z.at[g, :, g, :].set(w)
    return z


def _compress_weights(w1, w2, pos):
    w1r = w1.reshape(2, CMP_STRIDE, HEAD_DIM, CMP_HIDDEN)
    halves = []
    for half in range(2):
        z = jnp.zeros((CMP_STRIDE, NSA_GROUPS, HEAD_DIM, NSA_GROUPS, CMP_HIDDEN), F32)
        for g in range(NSA_GROUPS):
            z = z.at[:, g, :, g, :].set(w1r[half])
        halves.append(z.reshape(CMP_STRIDE * KV_DIM, NSA_GROUPS * CMP_HIDDEN))
    w1p = jnp.stack(halves).astype(BF16)
    w2p = _pair_groups(w2).reshape(NSA_GROUPS * CMP_HIDDEN, KV_DIM).astype(BF16)
    posr = pos.reshape(2, CMP_STRIDE, 1, HEAD_DIM)
    posp = jnp.broadcast_to(posr, (2, CMP_STRIDE, NSA_GROUPS, HEAD_DIM)).reshape(2, CMP_STRIDE * KV_DIM)
    return w1p, w2p, posp


def kernel(x, attn_norm_g, ffn_norm_g, w_in, w_out, nsa_q_norm_g, nsa_k_norm_g, cmp_pos_k, cmp_pos_v,
           cmp_wk1, cmp_wk2, cmp_wv1, cmp_wv2, nsa_out_norm_g, hgrn_lb_logits, hgrn_out_norm_g,
           moe_w_group, moe_b_group, moe_w_expert, moe_b_expert, moe_w1, moe_w3, moe_w2):
    batch, seq, _ = x.shape
    n_tok = batch * seq
    assert seq % SEL_CHUNK == 0 and seq >= WINDOW + 128 and seq // SEL_BLOCK <= HEAD_DIM
    depth = w_in.shape[0]
    lb_all = jnp.cumsum(jax.nn.softmax(hgrn_lb_logits.astype(F32), axis=0), axis=0)
    tm = min(512, n_tok)
    h = x.reshape(n_tok, D_MODEL)
    for layer in range(depth):
        wl = w_in[layer]
        gate_cols = wl[:, QKV_COLS:QKV_COLS + NSA_HEADS * NSA_BRANCHES]
        w_in_p = jnp.concatenate(
            [wl[:, :QKV_COLS], wl[:, QKV_COLS + NSA_HEADS * NSA_BRANCHES:],
             jnp.pad(gate_cols, ((0, 0), (0, LANES - NSA_HEADS * NSA_BRANCHES)))], axis=1).astype(BF16)
        gq128 = jnp.tile(nsa_q_norm_g[layer], 2)[None, :]
        gk128 = jnp.tile(nsa_k_norm_g[layer], (1, 2))
        g_attn = attn_norm_g[layer][None, :]

        qt, ksel, vselt, kwin, vwint, kc_raw, vc_raw, gates_t, hg = _inproj(
            h, g_attn, w_in_p, gq128, gk128, seq=seq, tm=tm)

        wk1p, wk2p, poskp = _compress_weights(cmp_wk1[layer], cmp_wk2[layer], cmp_pos_k[layer])
        wv1p, wv2p, posvp = _compress_weights(cmp_wv1[layer], cmp_wv2[layer], cmp_pos_v[layer])
        nrow = seq // CMP_STRIDE
        kcmp, vcmpt = _compress(kc_raw.reshape(batch, nrow, CMP_STRIDE * KV_DIM),
                                vc_raw.reshape(batch, nrow, CMP_STRIDE * KV_DIM),
                                poskp, posvp, wk1p, wv1p, wk2p, wv2p, gk128)

        o_nsa = _attention(qt, kcmp, vcmpt, ksel, vselt, kwin, vwint, gates_t,
                           batch=batch, seq=seq, tq=256)
        o_hgrn = _hgrn(hg, lb_all[layer][None, :], hgrn_out_norm_g[layer][None, :],
                       batch=batch, seq=seq, tc=min(512, seq))

        wo = w_out[layer].astype(BF16)
        wr = jnp.concatenate([moe_w_group[layer], moe_w_expert[layer]], axis=1)
        wr = jnp.pad(wr, ((0, 0), (0, LANES - wr.shape[1])))
        wr_hi = wr.astype(BF16)
        wr_lo = (wr - wr_hi.astype(F32)).astype(BF16)
        br = jnp.pad(jnp.concatenate([moe_b_group[layer], moe_b_expert[layer]]),
                     (0, LANES - MOE_GROUPS - N_EXPERTS))[None, :]
        h, hn, route = _outproj(o_nsa, o_hgrn, h, nsa_out_norm_g[layer][None, :],
                                wo[:NSA_WIDTH], wo[NSA_WIDTH:], ffn_norm_g[layer][None, :],
                                jnp.stack([wr_hi, wr_lo]), br, tm=tm)

        e_flat = route[:, 0:2].astype(I32).reshape(-1)
        n_asg = e_flat.shape[0]
        onehot = (e_flat[:, None] == jnp.arange(N_EXPERTS)[None, :]).astype(I32)
        rank = jnp.take_along_axis(jnp.cumsum(onehot, axis=0) - onehot, e_flat[:, None], axis=1)[:, 0]
        counts = jnp.sum(onehot, axis=0)
        padded = (counts + MOE_BLOCK - 1) // MOE_BLOCK * MOE_BLOCK
        pend = jnp.cumsum(padded)
        dest = (pend - padded)[e_flat] + rank
        p_rows = n_asg + N_EXPERTS * MOE_BLOCK
        nblk = p_rows // MOE_BLOCK
        src_asg = jnp.full((p_rows,), -1, I32).at[dest].set(jnp.arange(n_asg, dtype=I32))
        blk_e = jnp.minimum(jnp.sum(jnp.arange(nblk)[:, None] * MOE_BLOCK >= pend[None, :], axis=1),
                            N_EXPERTS - 1).astype(I32)
        n_used = (pend[-1] // MOE_BLOCK).astype(I32)[None]
        y2 = _experts(blk_e, n_used, src_asg, hn, moe_w1[layer].astype(BF16), moe_w3[layer].astype(BF16),
                      moe_w2[layer].astype(BF16))
        h = _combine(h, y2, route, tm=min(256, n_tok))
    return h.reshape(batch, seq, D_MODEL)
```

```python
import functools

import jax
import jax.numpy as jnp
import numpy as np
from jax import lax
from jax.experimental import pallas as pl
from jax.experimental.pallas import tpu as pltpu

F32 = jnp.float32
BF16 = jnp.bfloat16
I32 = jnp.int32

D_MODEL = 1024
NSA_HEADS = 8
NSA_GROUPS = 2
NSA_REP = NSA_HEADS // NSA_GROUPS
HEAD_DIM = 64
NSA_WIDTH = NSA_HEADS * HEAD_DIM
KV_DIM = NSA_GROUPS * HEAD_DIM
NSA_BRANCHES = 3
CMP_STRIDE = 16
CMP_LEN = 32
CMP_HIDDEN = 256
SEL_BLOCK = 64
SEL_TOPN = 16
WINDOW = 512
HGRN_WIDTH = D_MODEL - NSA_WIDTH
HGRN_HEADS = 4
HGRN_DK = 128
HGRN_CHUNK = 64
HGRN_SUB = 16
MOE_GROUPS = 4
EXPERTS_PER_GROUP = 8
N_EXPERTS = MOE_GROUPS * EXPERTS_PER_GROUP
EXPERT_FF = 512
MOE_BLOCK = 256
RMS_EPS = 1e-6
NEG_INF = -1e30

LANES = 128
QKV_COLS = NSA_WIDTH + 6 * KV_DIM
HG_COLS = 4 * HGRN_WIDTH
IN_COLS_PAD = QKV_COLS + HG_COLS + LANES
GATE_ROWS = 32
VT_BLOCK = 128
LOG2E = 1.4426950408889634
VMEM_LIMIT = 56 * 1024 * 1024


def _dot(a, b):
    return jnp.dot(a, b, preferred_element_type=F32)


def _dot_nt(a, b):
    return lax.dot_general(a, b, (((1,), (1,)), ((), ())), preferred_element_type=F32)


def _dot_tn(a, b):
    return lax.dot_general(a, b, (((0,), (0,)), ((), ())), preferred_element_type=F32)


def _split3(x):
    a = x.astype(BF16)
    r = x - a.astype(F32)
    b = r.astype(BF16)
    c = (r - b.astype(F32)).astype(BF16)
    return a, b, c


def _silu(x):
    return x * jax.nn.sigmoid(x)


def _params(sem):
    return pltpu.CompilerParams(dimension_semantics=sem, vmem_limit_bytes=VMEM_LIMIT)


def _seg_rms(blk, low, gain):
    sq = blk * blk
    s_lo = jnp.sum(jnp.where(low, sq, 0.0), axis=-1, keepdims=True)
    s_hi = jnp.sum(jnp.where(low, 0.0, sq), axis=-1, keepdims=True)
    inv = jnp.where(low, lax.rsqrt(s_lo * (1.0 / HEAD_DIM) + RMS_EPS),
                    lax.rsqrt(s_hi * (1.0 / HEAD_DIM) + RMS_EPS))
    return blk * inv * gain


def _inproj_body(x_ref, g_ref, w_ref, gq_ref, gk_ref,
                 qt_ref, ksel_ref, vselt_ref, kwin_ref, vwint_ref, kc_ref, vc_ref,
                 gates_ref, hg_ref, *, tm, seq):
    x = x_ref[...]
    ms = jnp.mean(x * x, axis=-1, keepdims=True)
    n = (x * lax.rsqrt(ms + RMS_EPS) * g_ref[...]).astype(BF16)
    y = _dot(n, w_ref[...])

    lane = lax.broadcasted_iota(I32, (1, LANES), 1)
    low = lane < HEAD_DIM
    gq = gq_ref[...]
    scale = HEAD_DIM ** -0.5 * LOG2E
    for j in range(NSA_HEADS // 2):
        nrm_t = (_seg_rms(y[:, LANES * j:LANES * (j + 1)], low, gq) * scale).T.astype(BF16)
        qt_ref[2 * j] = nrm_t[0:HEAD_DIM]
        qt_ref[2 * j + 1] = nrm_t[HEAD_DIM:]

    def kv_block(idx):
        c0 = NSA_WIDTH + LANES * idx
        return y[:, c0:c0 + LANES]

    def split_groups(blk, hi_fill):
        g0 = jnp.where(low, blk, hi_fill)
        g1 = jnp.where(low, pltpu.roll(blk, HEAD_DIM, 1), hi_fill)
        return g0.astype(BF16), g1.astype(BF16)

    def store_transposed(vt_ref, blk):
        blk_t = blk.T.astype(BF16)
        for g in range(NSA_GROUPS):
            for c in range(tm // VT_BLOCK):
                vt_ref[g, c] = blk_t[g * HEAD_DIM:(g + 1) * HEAD_DIM, c * VT_BLOCK:(c + 1) * VT_BLOCK]

    kc_ref[...] = kv_block(0).astype(BF16)
    vc_ref[...] = kv_block(1).astype(BF16)

    row = lax.broadcasted_iota(I32, (tm, LANES), 0)
    t = (pl.program_id(0) * tm) % seq + row
    onehot = jnp.where(lane - HEAD_DIM == lax.shift_right_logical(t, 6), 1.0, 0.0)
    ks = _seg_rms(kv_block(2), low, gk_ref[1:2, :])
    ksel_ref[0], ksel_ref[1] = split_groups(ks, onehot)
    store_transposed(vselt_ref, kv_block(3))
    kw = _seg_rms(kv_block(4), low, gk_ref[2:3, :])
    kwin_ref[0], kwin_ref[1] = split_groups(kw, 0.0)
    store_transposed(vwint_ref, kv_block(5))

    hg_ref[...] = y[:, QKV_COLS:QKV_COLS + HG_COLS].astype(BF16)
    gates_ref[...] = y[:, QKV_COLS + HG_COLS:].T[0:GATE_ROWS]


def _inproj(x2, g_attn, w_in_p, gq128, gk128, *, seq, tm):
    n_tok = x2.shape[0]
    grid = (n_tok // tm,)
    row_spec = lambda cols: pl.BlockSpec((tm, cols), lambda i: (i, 0))
    k_spec = pl.BlockSpec((NSA_GROUPS, tm, LANES), lambda i: (0, i, 0))
    vt_spec = pl.BlockSpec((NSA_GROUPS, tm // VT_BLOCK, HEAD_DIM, VT_BLOCK), lambda i: (0, i, 0, 0))
    vt_shape = jax.ShapeDtypeStruct((NSA_GROUPS, n_tok // VT_BLOCK, HEAD_DIM, VT_BLOCK), BF16)
    full = lambda a: pl.BlockSpec(a.shape, lambda i: (0,) * a.ndim)
    out_shape = (
        jax.ShapeDtypeStruct((NSA_HEADS, HEAD_DIM, n_tok), BF16),
        jax.ShapeDtypeStruct((NSA_GROUPS, n_tok, LANES), BF16),
        vt_shape,
        jax.ShapeDtypeStruct((NSA_GROUPS, n_tok, LANES), BF16),
        vt_shape,
        jax.ShapeDtypeStruct((n_tok, LANES), BF16),
        jax.ShapeDtypeStruct((n_tok, LANES), BF16),
        jax.ShapeDtypeStruct((GATE_ROWS, n_tok), F32),
        jax.ShapeDtypeStruct((n_tok, HG_COLS), BF16),
    )
    out_specs = (pl.BlockSpec((NSA_HEADS, HEAD_DIM, tm), lambda i: (0, 0, i)),
                 k_spec, vt_spec, k_spec, vt_spec, row_spec(LANES), row_spec(LANES),
                 pl.BlockSpec((GATE_ROWS, tm), lambda i: (0, i)), row_spec(HG_COLS))
    return pl.pallas_call(
        functools.partial(_inproj_body, tm=tm, seq=seq),
        grid=grid,
        in_specs=[row_spec(D_MODEL), full(g_attn), full(w_in_p), full(gq128), full(gk128)],
        out_specs=out_specs,
        out_shape=out_shape,
        compiler_params=_params(("parallel",)),
        name="inproj",
    )(x2, g_attn, w_in_p, gq128, gk128)


def _compress_body(kc_ref, vc_ref, posk_ref, posv_ref, wk1_ref, wv1_ref, wk2_ref, wv2_ref,
                   gk_ref, kco_ref, vco_ref, *, nrow):
    lane = lax.broadcasted_iota(I32, (1, LANES), 1)
    low = lane < HEAD_DIM

    def mlp(x_ref, pos_ref, w1_ref, w2_ref):
        x = x_ref[0].astype(F32)
        ha = _dot((x + pos_ref[0:1, :]).astype(BF16), w1_ref[0])
        hb = _dot((x + pos_ref[1:2, :]).astype(BF16), w1_ref[1])
        hid = _silu(ha + pltpu.roll(hb, nrow - 1, 0))
        return _dot(hid.astype(BF16), w2_ref[...])

    def split_groups(blk):
        g0 = jnp.where(low, blk, 0.0)
        g1 = jnp.where(low, pltpu.roll(blk, HEAD_DIM, 1), 0.0)
        return g0.astype(BF16), g1.astype(BF16)

    kc = _seg_rms(mlp(kc_ref, posk_ref, wk1_ref, wk2_ref), low, gk_ref[0:1, :])
    kco_ref[0, 0], kco_ref[0, 1] = split_groups(kc)
    vc_t = mlp(vc_ref, posv_ref, wv1_ref, wv2_ref).T.astype(BF16)
    vco_ref[0, 0] = vc_t[0:HEAD_DIM]
    vco_ref[0, 1] = vc_t[HEAD_DIM:]


def _compress(kc3, vc3, posk, posv, wk1, wv1, wk2, wv2, gk128):
    batch, nrow, width = kc3.shape
    full = lambda a: pl.BlockSpec(a.shape, lambda b: (0,) * a.ndim)
    in_spec = pl.BlockSpec((1, nrow, width), lambda b: (b, 0, 0))
    k_spec = pl.BlockSpec((1, NSA_GROUPS, nrow, LANES), lambda b: (b, 0, 0, 0))
    vt_spec = pl.BlockSpec((1, NSA_GROUPS, HEAD_DIM, nrow), lambda b: (b, 0, 0, 0))
    return pl.pallas_call(
        functools.partial(_compress_body, nrow=nrow),
        grid=(batch,),
        in_specs=[in_spec, in_spec, full(posk), full(posv), full(wk1), full(wv1),
                  full(wk2), full(wv2), full(gk128)],
        out_specs=(k_spec, vt_spec),
        out_shape=(jax.ShapeDtypeStruct((batch, NSA_GROUPS, nrow, LANES), BF16),
                   jax.ShapeDtypeStruct((batch, NSA_GROUPS, HEAD_DIM, nrow), BF16)),
        compiler_params=_params(("parallel",)),
        name="compress",
    )(kc3, vc3, posk, posv, wk1, wv1, wk2, wv2, gk128)


SEL_CHUNK = 512


def _vt_cols(vt_ref, start, n):
    b0 = start // VT_BLOCK
    return jnp.concatenate([vt_ref[0, b0 + i] for i in range(n // VT_BLOCK)], axis=1)


def _attn_body(qt_ref, kc_ref, vct_ref, ks_ref, vst_ref, kw_ref, vwt_ref, gt_ref, o_ref,
               *, tq, seq, ncmp):
    g = pl.program_id(1)
    t0 = pl.program_id(2) * tq
    cols = NSA_REP * tq
    qt = jnp.concatenate([qt_ref[r] for r in range(NSA_REP)], axis=1)
    q_pad = jnp.concatenate([qt, jnp.zeros((HEAD_DIM, cols), BF16)], axis=0)
    tcol = t0 + (lax.broadcasted_iota(I32, (1, cols), 1) & (tq - 1))
    tlane = t0 + lax.broadcasted_iota(I32, (1, tq), 1)
    per_head = lambda a: jnp.concatenate([a] * NSA_REP, axis=1)

    s = _dot(kc_ref[0, 0], q_pad)
    cpos = lax.broadcasted_iota(I32, (ncmp, 1), 0) * CMP_STRIDE + (CMP_LEN - 1)
    cmask = cpos <= tcol
    s = jnp.where(cmask, s, NEG_INF)
    e = jnp.where(cmask, jnp.exp2(s - jnp.max(s, axis=0, keepdims=True)), 0.0)
    den = jnp.sum(e, axis=0, keepdims=True)
    p_c = e / jnp.where(den > 0.0, den, 1.0)
    o_c = _dot(vct_ref[0, 0], p_c.astype(BF16))

    p_sum = p_c[:, 0:tq]
    for r in range(1, NSA_REP):
        p_sum = p_sum + p_c[:, r * tq:(r + 1) * tq]
    nblk = seq // SEL_BLOCK
    jj = lax.broadcasted_iota(I32, (nblk, ncmp), 0) * SEL_BLOCK
    cc = lax.broadcasted_iota(I32, (nblk, ncmp), 1) * CMP_STRIDE
    ov = jnp.maximum(jnp.minimum(cc + CMP_LEN, jj + SEL_BLOCK) - jnp.maximum(cc, jj), 0)
    ov = (ov.astype(F32) * (1.0 / CMP_LEN)).astype(BF16)
    p1, p2, p3 = _split3(p_sum)
    imp = _dot(ov, p1) + _dot(ov, p2) + _dot(ov, p3)

    jb = lax.broadcasted_iota(I32, (nblk, tq), 0)
    cur = lax.shift_right_logical(t0 + lax.broadcasted_iota(I32, (nblk, tq), 1), 6)
    valid = jb <= cur
    forced = (jb == cur) | (jb == 0)
    imp = jnp.where(forced, jnp.inf, jnp.where(valid, imp, -jnp.inf))
    rank = jnp.zeros((nblk, tq), F32)
    for j2 in range(nblk):
        other = imp[j2:j2 + 1, :]
        ahead = (other > imp) | ((other == imp) & (jb > j2))
        rank = rank + jnp.where(ahead, 1.0, 0.0)
    sel = (rank < float(min(SEL_TOPN, nblk))) & valid
    bias = jnp.where(sel, 0.0, NEG_INF).astype(BF16)
    if nblk < HEAD_DIM:
        bias = jnp.concatenate([bias, jnp.zeros((HEAD_DIM - nblk, tq), BF16)], axis=0)
    q_sel = jnp.concatenate([qt, per_head(bias)], axis=0)

    def sel_step(start, extra, carry):
        m_old, l_old, acc = carry
        sc = _dot(ks_ref[0, pl.ds(start, SEL_CHUNK), :], q_sel)
        if extra is not None:
            sc = sc + extra
        m_new = jnp.maximum(m_old, jnp.max(sc, axis=0, keepdims=True))
        alpha = jnp.exp2(m_old - m_new)
        p = jnp.exp2(sc - m_new)
        l_new = alpha * l_old + jnp.sum(p, axis=0, keepdims=True)
        acc = alpha * acc + _dot(_vt_cols(vst_ref, start, SEL_CHUNK), p.astype(BF16))
        return m_new, l_new, acc

    n_full = t0 // SEL_CHUNK
    init = (jnp.full((1, cols), -3.0e38, F32), jnp.zeros((1, cols), F32), jnp.zeros((HEAD_DIM, cols), F32))
    carry = lax.fori_loop(
        0, n_full, lambda ci, c: sel_step(pl.multiple_of(ci * SEL_CHUNK, SEL_CHUNK), None, c), init)
    tail0 = pl.multiple_of(n_full * SEL_CHUNK, SEL_CHUNK)
    kpos = tail0 + lax.broadcasted_iota(I32, (SEL_CHUNK, 1), 0)
    causal = jnp.where(kpos <= tlane, 0.0, NEG_INF)
    _, l_s, acc_s = sel_step(tail0, per_head(causal), carry)
    o_s = acc_s / l_s

    band = WINDOW + tq
    w0 = pl.multiple_of(jnp.maximum(t0 - WINDOW, 0), tq)
    sw = _dot(kw_ref[0, pl.ds(w0, band), :], q_pad)
    dist = tlane - (w0 + lax.broadcasted_iota(I32, (band, 1), 0))
    in_win = lax.bitcast_convert_type(dist, jnp.uint32) < jnp.uint32(WINDOW)
    sw = sw + per_head(jnp.where(in_win, 0.0, NEG_INF))
    pw = jnp.exp2(sw - jnp.max(sw, axis=0, keepdims=True))
    o_w = _dot(_vt_cols(vwt_ref, w0, band), pw.astype(BF16)) / jnp.sum(pw, axis=0, keepdims=True)

    def gate(r, br):
        col = (g * NSA_REP + r) * NSA_BRANCHES + br
        return jax.nn.sigmoid(gt_ref[pl.ds(col, 1), :])

    heads = []
    for r in range(NSA_REP):
        sl = slice(r * tq, (r + 1) * tq)
        heads.append(gate(r, 0) * o_c[:, sl] + gate(r, 1) * o_s[:, sl] + gate(r, 2) * o_w[:, sl])
    o_ref[...] = jnp.concatenate(heads, axis=0).T.astype(BF16)


def _attention(qt, kcmp, vcmpt, ksel, vselt, kwin, vwint, gates_t, *, batch, seq, tq):
    n_tok = batch * seq
    ncmp = kcmp.shape[2]
    nq = seq // tq
    nvb = seq // VT_BLOCK
    k_spec = pl.BlockSpec((1, seq, LANES), lambda b, g, i: (g, b, 0))
    vt_spec = pl.BlockSpec((1, nvb, HEAD_DIM, VT_BLOCK), lambda b, g, i: (g, b, 0, 0))
    return pl.pallas_call(
        functools.partial(_attn_body, tq=tq, seq=seq, ncmp=ncmp),
        grid=(batch, NSA_GROUPS, nq),
        in_specs=[
            pl.BlockSpec((NSA_REP, HEAD_DIM, tq), lambda b, g, i: (g, 0, b * nq + i)),
            pl.BlockSpec((1, 1, ncmp, LANES), lambda b, g, i: (b, g, 0, 0)),
            pl.BlockSpec((1, 1, HEAD_DIM, ncmp), lambda b, g, i: (b, g, 0, 0)),
            k_spec, vt_spec, k_spec, vt_spec,
            pl.BlockSpec((GATE_ROWS, tq), lambda b, g, i: (0, b * nq + i)),
        ],
        out_specs=pl.BlockSpec((tq, NSA_REP * HEAD_DIM), lambda b, g, i: (b * nq + i, g)),
        out_shape=jax.ShapeDtypeStruct((n_tok, NSA_WIDTH), BF16),
        compiler_params=_params(("parallel", "parallel", "arbitrary")),
        name="nsa_attention",
    )(qt, kcmp, vcmpt, ksel, vselt, kwin, vwint, gates_t)


def _hgrn_body(q_ref, f_ref, i_ref, g_ref, lb_ref, gn_ref, o_ref, st_ref, *, tc):
    C, SUB = HGRN_CHUNK, HGRN_SUB

    @pl.when(pl.program_id(2) == 0)
    def _():
        st_ref[...] = jnp.zeros_like(st_ref)

    r = lax.broadcasted_iota(I32, (C, C), 0)
    c = lax.broadcasted_iota(I32, (C, C), 1)
    rs, cs = r >> 4, c >> 4
    one = lambda m: jnp.where(m, 1.0, 0.0).astype(BF16)
    sum_mats = jnp.concatenate([one(c <= r), one((c <= r) & (rs == cs)),
                                one((c > r) & (rs == cs)), one(c > r)], axis=0)
    row = lax.broadcasted_iota(I32, (C, 1), 0)
    lb = lb_ref[...]
    gn = gn_ref[...]

    def chunk(ci, carry):
        sl = pl.ds(pl.multiple_of(ci * C, C), C)
        q = _silu(q_ref[sl, :].astype(F32))
        fg = lb + (1.0 - lb) * jax.nn.sigmoid(f_ref[sl, :].astype(F32))
        k = 1.0 - fg
        v = i_ref[sl, :].astype(F32)
        l1, l2, l3 = _split3(jnp.log(fg))
        sums = _dot(sum_mats, l1) + _dot(sum_mats, l2) + _dot(sum_mats, l3)
        a_full = sums[0:C]
        a_sub = sums[C:2 * C]
        b_sub = sums[2 * C:3 * C]
        b_full = sums[3 * C:4 * C]

        out = jnp.sum(q * k, axis=-1, keepdims=True) * v
        for d in range(1, SUB):
            ok = (row & (SUB - 1)) >= d
            arg = jnp.where(ok, a_sub - pltpu.roll(a_sub, d, 0), NEG_INF)
            w = jnp.sum(q * pltpu.roll(k, d, 0) * jnp.exp(arg), axis=-1, keepdims=True)
            out = out + w * pltpu.roll(v, d, 0)

        qd = q * jnp.exp(a_sub)
        kd = k * jnp.exp(b_sub)
        tot1 = a_sub[2 * SUB - 1:2 * SUB, :]
        tot2 = a_sub[3 * SUB - 1:3 * SUB, :]
        q_mid = qd * jnp.where(row >= 3 * SUB, jnp.exp(tot2), 1.0)
        k_mid = kd * jnp.where(row < SUB, jnp.exp(tot1), 1.0)
        a_adj = _dot_nt(qd.astype(BF16), kd.astype(BF16))
        a_mid = _dot_nt(q_mid.astype(BF16), k_mid.astype(BF16))
        a_off = jnp.where(rs == cs + 1, a_adj, jnp.where(rs - cs >= 2, a_mid, 0.0))
        vb = v.astype(BF16)
        out = out + _dot(a_off.astype(BF16), vb)

        st = st_ref[...]
        out = out + _dot_nt((q * jnp.exp(a_full)).astype(BF16), st.astype(BF16))
        k_dec = (k * jnp.exp(b_full)).astype(BF16)
        st_ref[...] = st * jnp.exp(a_full[C - 1:C, :]) + _dot_tn(vb, k_dec)

        ms = jnp.mean(out * out, axis=-1, keepdims=True)
        on = out * lax.rsqrt(ms + RMS_EPS) * gn
        o_ref[sl, :] = (on * _silu(g_ref[sl, :].astype(F32))).astype(BF16)
        return carry

    lax.fori_loop(0, tc // C, chunk, 0, unroll=4)


def _hgrn(hg, lb, gn, *, batch, seq, tc):
    n_tok = batch * seq
    nt = seq // tc
    part = lambda p: pl.BlockSpec((tc, HGRN_DK), lambda b, h, i, p=p: (b * nt + i, p * HGRN_HEADS + h))
    return pl.pallas_call(
        functools.partial(_hgrn_body, tc=tc),
        grid=(batch, HGRN_HEADS, nt),
        in_specs=[part(0), part(1), part(2), part(3),
                  pl.BlockSpec((1, HGRN_DK), lambda b, h, i: (0, h)),
                  pl.BlockSpec((1, HGRN_DK), lambda b, h, i: (0, 0))],
        out_specs=pl.BlockSpec((tc, HGRN_DK), lambda b, h, i: (b * nt + i, h)),
        out_shape=jax.ShapeDtypeStruct((n_tok, HGRN_WIDTH), BF16),
        scratch_shapes=[pltpu.VMEM((HGRN_DK, HGRN_DK), F32)],
        compiler_params=_params(("parallel", "parallel", "arbitrary")),
        name="hgrn2",
    )(hg, hg, hg, hg, lb, gn)


def _outproj_body(on_ref, oh_ref, x_ref, gno_ref, wo1_ref, wo2_ref, gf_ref, wr_ref, br_ref,
                  h_ref, hn_ref, route_ref):
    a = on_ref[...].astype(F32)
    ms = jnp.mean(a * a, axis=-1, keepdims=True)
    an = (a * lax.rsqrt(ms + RMS_EPS) * gno_ref[...]).astype(BF16)
    h = x_ref[...] + _dot(an, wo1_ref[...]) + _dot(oh_ref[...], wo2_ref[...])
    h_ref[...] = h
    ms2 = jnp.mean(h * h, axis=-1, keepdims=True)
    hn = h * lax.rsqrt(ms2 + RMS_EPS) * gf_ref[...]
    tm = hn.shape[0]
    for s in range(D_MODEL // LANES):
        hn_ref[pl.ds(s, tm, stride=D_MODEL // LANES), :] = hn[:, s * LANES:(s + 1) * LANES]

    hn_hi = hn.astype(BF16)
    hn_lo = (hn - hn_hi.astype(F32)).astype(BF16)
    lg = (_dot(hn_hi, wr_ref[0]) + _dot(hn_lo, wr_ref[0]) + _dot(hn_hi, wr_ref[1])) + br_ref[...]

    lane = lax.broadcasted_iota(I32, (1, LANES), 1).astype(F32)
    first = lambda hit: jnp.min(jnp.where(hit, lane, 1e9), axis=-1, keepdims=True)
    gmask = lane < MOE_GROUPS
    lgm = jnp.where(gmask, lg, -jnp.inf)
    mg = jnp.max(lgm, axis=-1, keepdims=True)
    pg_top = 1.0 / jnp.sum(jnp.where(gmask, jnp.exp(lgm - mg), 0.0), axis=-1, keepdims=True)
    gidx = first(lgm == mg)
    eloc = lane - MOE_GROUPS
    emask = (eloc >= 0) & (eloc < N_EXPERTS) & (jnp.floor(eloc * (1.0 / EXPERTS_PER_GROUP)) == gidx)
    le1 = jnp.where(emask, lg, -jnp.inf)
    m1 = jnp.max(le1, axis=-1, keepdims=True)
    i1 = first(le1 == m1)
    le2 = jnp.where(lane == i1, -jnp.inf, le1)
    m2 = jnp.max(le2, axis=-1, keepdims=True)
    i2 = first(le2 == m2)
    e2 = jnp.exp(m2 - m1)
    w1 = pg_top / (1.0 + e2)
    w2 = pg_top * e2 / (1.0 + e2)
    route_ref[...] = jnp.where(lane == 0, i1 - MOE_GROUPS,
                     jnp.where(lane == 1, i2 - MOE_GROUPS,
                     jnp.where(lane == 2, w1, jnp.where(lane == 3, w2, 0.0))))


def _outproj(o_nsa, o_hgrn, x2, gno, wo1, wo2, gf, wr, br, *, tm):
    n_tok = x2.shape[0]
    row_spec = lambda cols: pl.BlockSpec((tm, cols), lambda i: (i, 0))
    full = lambda a: pl.BlockSpec(a.shape, lambda i: (0,) * a.ndim)
    return pl.pallas_call(
        _outproj_body,
        grid=(n_tok // tm,),
        in_specs=[row_spec(NSA_WIDTH), row_spec(HGRN_WIDTH), row_spec(D_MODEL), full(gno),
                  full(wo1), full(wo2), full(gf), full(wr), full(br)],
        out_specs=(row_spec(D_MODEL),
                   pl.BlockSpec((tm * (D_MODEL // LANES), LANES), lambda i: (i, 0)),
                   row_spec(LANES)),
        out_shape=(jax.ShapeDtypeStruct((n_tok, D_MODEL), F32),
                   jax.ShapeDtypeStruct((n_tok * (D_MODEL // LANES), LANES), F32),
                   jax.ShapeDtypeStruct((n_tok, LANES), F32)),
        compiler_params=_params(("parallel",)),
        name="outproj_router",
    )(o_nsa, o_hgrn, x2, gno, wo1, wo2, gf, wr, br)


SLAB = D_MODEL // LANES
ROW_BITS = 17


def _expert_body(be_ref, nb_ref, src_ref, hn_ref, w1_ref, w3_ref, w2_ref, y_ref,
                 xbuf, ybuf, gsem, ssem, *, n_asg):
    i = pl.program_id(0)
    n_used = nb_ref[0]
    R = MOE_BLOCK
    last_blk = pl.num_programs(0) - 1

    def slab(r):
        return pl.ds(r * SLAB if isinstance(r, int) else pl.multiple_of(r * SLAB, SLAB), SLAB)

    def gather_row(blk, slot, r):
        tok = lax.shift_right_logical(src_ref[blk * R + r], ROW_BITS)
        pltpu.make_async_copy(hn_ref.at[tok], xbuf.at[slot, slab(r), :], gsem.at[slot]).start()

    def gather_wait(slot):
        pltpu.make_async_copy(xbuf.at[slot], xbuf.at[slot], gsem.at[slot]).wait()

    def scatter_slab(r, row):
        pltpu.make_async_copy(ybuf.at[slab(r), :], y_ref.at[row], ssem).start()

    def scatter_row(blk, r):
        scatter_slab(r, src_ref[blk * R + r] & ((1 << ROW_BITS) - 1))

    def scatter_wait():
        pltpu.make_async_copy(ybuf, ybuf, ssem).wait()

    def rolled(fn):
        def body(r, c):
            fn(r)
            return c
        lax.fori_loop(0, R, body, 0, unroll=8)

    @pl.when(i == 0)
    def _():
        ybuf[...] = jnp.zeros_like(ybuf)
        rolled(lambda r: scatter_slab(r, n_asg + r))
        scatter_wait()
        rolled(lambda r: gather_row(0, 0, r))

    @pl.when(i < n_used)
    def _():
        slot = i & 1
        gather_wait(slot)
        x = jnp.concatenate([xbuf[slot, pl.ds(s, R, stride=SLAB), :] for s in range(SLAB)], axis=1).astype(BF16)
        nxt = jnp.minimum(i + 1, n_used - 1)
        prev = jnp.where(i > 0, i - 1, last_blk)
        for r in range(R):
            gather_row(nxt, 1 - slot, r)
            scatter_row(prev, r)
        hid = _silu(_dot(x, w1_ref[0])) * _dot(x, w3_ref[0])
        y = _dot(hid.astype(BF16), w2_ref[0])
        scatter_wait()
        for s in range(SLAB):
            ybuf[pl.ds(s, R, stride=SLAB), :] = y[:, s * LANES:(s + 1) * LANES]

        @pl.when(i == n_used - 1)
        def _():
            rolled(lambda r: scatter_row(i, r))
            scatter_wait()
            gather_wait(1 - slot)


def _experts(blk_e, n_used, row_map, hn_slabs, w1, w3, w2):
    p_rows = row_map.shape[0]
    n_asg = 2 * hn_slabs.shape[0]
    nblk = p_rows // MOE_BLOCK
    wspec = lambda a: pl.BlockSpec((1,) + a.shape[1:], lambda i, be, nb, src: (be[i], 0, 0))
    grid_spec = pltpu.PrefetchScalarGridSpec(
        num_scalar_prefetch=3,
        grid=(nblk,),
        in_specs=[pl.BlockSpec(memory_space=pl.ANY), wspec(w1), wspec(w3), wspec(w2)],
        out_specs=pl.BlockSpec(memory_space=pl.ANY),
        scratch_shapes=[pltpu.VMEM((2, MOE_BLOCK * SLAB, LANES), F32), pltpu.VMEM((MOE_BLOCK * SLAB, LANES), F32),
                        pltpu.SemaphoreType.DMA((2,)), pltpu.SemaphoreType.DMA],
    )
    return pl.pallas_call(
        functools.partial(_expert_body, n_asg=n_asg),
        grid_spec=grid_spec,
        out_shape=jax.ShapeDtypeStruct((n_asg + MOE_BLOCK, SLAB, LANES), F32),
        compiler_params=_params(("arbitrary",)),
        name="experts",
    )(blk_e, n_used, row_map, hn_slabs, w1, w3, w2)


def _combine_body(h_ref, y0_ref, y1_ref, rt_ref, o_ref, *, tm):
    rt = rt_ref[...]
    w0, w1 = rt[:, 2:3], rt[:, 3:4]
    for s in range(SLAB):
        cols = slice(s * LANES, (s + 1) * LANES)
        y0 = y0_ref[pl.ds(s, tm, stride=SLAB), :]
        y1 = y1_ref[pl.ds(s, tm, stride=SLAB), :]
        o_ref[:, cols] = h_ref[:, cols] + (w0 * y0 + w1 * y1)


def _combine(h, y_slabs, route, *, tm):
    n_tok = h.shape[0]
    nb = n_tok // tm
    y2d = y_slabs.reshape(-1, LANES)
    return pl.pallas_call(
        functools.partial(_combine_body, tm=tm),
        grid=(nb,),
        in_specs=[pl.BlockSpec((tm, D_MODEL), lambda i: (i, 0)),
                  pl.BlockSpec((tm * SLAB, LANES), lambda i: (i, 0)),
                  pl.BlockSpec((tm * SLAB, LANES), lambda i: (nb + i, 0)),
                  pl.BlockSpec((tm, LANES), lambda i: (i, 0))],
        out_specs=pl.BlockSpec((tm, D_MODEL), lambda i: (i, 0)),
        out_shape=jax.ShapeDtypeStruct((n_tok, D_MODEL), F32),
        compiler_params=_params(("parallel",)),
        name="moe_combine",
    )(h, y2d, y2d, route)


def _pair_groups(w):
    rows, cols = w.shape
    z = jnp.zeros((NSA_GROUPS, rows, NSA_GROUPS, cols), w.dtype)
    for g in range(NSA_GROUPS):
        z = z.at[g, :, g, :].set(w)
    return z


def _compress_weights(w1, w2, pos):
    w1r = w1.reshape(2, CMP_STRIDE, HEAD_DIM, CMP_HIDDEN)
    halves = []
    for half in range(2):
        z = jnp.zeros((CMP_STRIDE, NSA_GROUPS, HEAD_DIM, NSA_GROUPS, CMP_HIDDEN), F32)
        for g in range(NSA_GROUPS):
            z = z.at[:, g, :, g, :].set(w1r[half])
        halves.append(z.reshape(CMP_STRIDE * KV_DIM, NSA_GROUPS * CMP_HIDDEN))
    w1p = jnp.stack(halves).astype(BF16)
    w2p = _pair_groups(w2).reshape(NSA_GROUPS * CMP_HIDDEN, KV_DIM).astype(BF16)
    posr = pos.reshape(2, CMP_STRIDE, 1, HEAD_DIM)
    posp = jnp.broadcast_to(posr, (2, CMP_STRIDE, NSA_GROUPS, HEAD_DIM)).reshape(2, CMP_STRIDE * KV_DIM)
    return w1p, w2p, posp


def kernel(x, attn_norm_g, ffn_norm_g, w_in, w_out, nsa_q_norm_g, nsa_k_norm_g, cmp_pos_k, cmp_pos_v,
           cmp_wk1, cmp_wk2, cmp_wv1, cmp_wv2, nsa_out_norm_g, hgrn_lb_logits, hgrn_out_norm_g,
           moe_w_group, moe_b_group, moe_w_expert, moe_b_expert, moe_w1, moe_w3, moe_w2):
    batch, seq, _ = x.shape
    n_tok = batch * seq
    assert seq % SEL_CHUNK == 0 and seq >= WINDOW + 128 and seq // SEL_BLOCK <= HEAD_DIM
    depth = w_in.shape[0]
    lb_all = jnp.cumsum(jax.nn.softmax(hgrn_lb_logits.astype(F32), axis=0), axis=0)
    tm = min(512, n_tok)
    h = x.reshape(n_tok, D_MODEL)
    for layer in range(depth):
        wl = w_in[layer]
        gate_cols = wl[:, QKV_COLS:QKV_COLS + NSA_HEADS * NSA_BRANCHES]
        w_in_p = jnp.concatenate(
            [wl[:, :QKV_COLS], wl[:, QKV_COLS + NSA_HEADS * NSA_BRANCHES:],
             jnp.pad(gate_cols, ((0, 0), (0, LANES - NSA_HEADS * NSA_BRANCHES)))], axis=1).astype(BF16)
        gq128 = jnp.tile(nsa_q_norm_g[layer], 2)[None, :]
        gk128 = jnp.tile(nsa_k_norm_g[layer], (1, 2))
        g_attn = attn_norm_g[layer][None, :]

        qt, ksel, vselt, kwin, vwint, kc_raw, vc_raw, gates_t, hg = _inproj(
            h, g_attn, w_in_p, gq128, gk128, seq=seq, tm=tm)

        wk1p, wk2p, poskp = _compress_weights(cmp_wk1[layer], cmp_wk2[layer], cmp_pos_k[layer])
        wv1p, wv2p, posvp = _compress_weights(cmp_wv1[layer], cmp_wv2[layer], cmp_pos_v[layer])
        nrow = seq // CMP_STRIDE
        kcmp, vcmpt = _compress(kc_raw.reshape(batch, nrow, CMP_STRIDE * KV_DIM),
                                vc_raw.reshape(batch, nrow, CMP_STRIDE * KV_DIM),
                                poskp, posvp, wk1p, wv1p, wk2p, wv2p, gk128)

        o_nsa = _attention(qt, kcmp, vcmpt, ksel, vselt, kwin, vwint, gates_t,
                           batch=batch, seq=seq, tq=256)
        o_hgrn = _hgrn(hg, lb_all[layer][None, :], hgrn_out_norm_g[layer][None, :],
                       batch=batch, seq=seq, tc=min(512, seq))

        wo = w_out[layer].astype(BF16)
        wr = jnp.concatenate([moe_w_group[layer], moe_w_expert[layer]], axis=1)
        wr = jnp.pad(wr, ((0, 0), (0, LANES - wr.shape[1])))
        wr_hi = wr.astype(BF16)
        wr_lo = (wr - wr_hi.astype(F32)).astype(BF16)
        br = jnp.pad(jnp.concatenate([moe_b_group[layer], moe_b_expert[layer]]),
                     (0, LANES - MOE_GROUPS - N_EXPERTS))[None, :]
        h, hn, route = _outproj(o_nsa, o_hgrn, h, nsa_out_norm_g[layer][None, :],
                                wo[:NSA_WIDTH], wo[NSA_WIDTH:], ffn_norm_g[layer][None, :],
                                jnp.stack([wr_hi, wr_lo]), br, tm=tm)

        e_flat = route[:, 0:2].astype(I32).reshape(-1)
        n_asg = e_flat.shape[0]
        assert 2 * n_tok + MOE_BLOCK <= 1 << ROW_BITS and n_tok <= 1 << (32 - ROW_BITS)
        order = jnp.argsort(e_flat, stable=True).astype(I32)
        counts = jnp.sum((e_flat[:, None] == jnp.arange(N_EXPERTS)[None, :]).astype(I32), axis=0)
        start = jnp.cumsum(counts) - counts
        padded = (counts + MOE_BLOCK - 1) // MOE_BLOCK * MOE_BLOCK
        pend = jnp.cumsum(padded)
        p_rows = n_asg + N_EXPERTS * MOE_BLOCK
        nblk = p_rows // MOE_BLOCK
        blk_e = jnp.minimum(jnp.sum(jnp.arange(nblk)[:, None] * MOE_BLOCK >= pend[None, :], axis=1),
                            N_EXPERTS - 1).astype(I32)
        n_used = (pend[-1] // MOE_BLOCK).astype(I32)[None]
        p_idx = jnp.arange(p_rows, dtype=I32)
        e_row = blk_e[p_idx // MOE_BLOCK]
        r_row = p_idx - (pend - padded)[e_row]
        live = r_row < counts[e_row]
        asg = order[jnp.clip(start[e_row] + r_row, 0, n_asg - 1)]
        tok = asg >> 1
        out_row = jnp.where(live, (asg & 1) * n_tok + tok, n_asg + (p_idx % MOE_BLOCK))
        row_map = (jnp.where(live, tok, 0) << ROW_BITS) | out_row
        y_slabs = _experts(blk_e, n_used, row_map, hn.reshape(n_tok, SLAB, LANES),
                           moe_w1[layer].astype(BF16), moe_w3[layer].astype(BF16), moe_w2[layer].astype(BF16))
        h = _combine(h, y_slabs, route, tm=min(256, n_tok))
    return h.reshape(batch, seq, D_MODEL)
```

```python
import functools

import jax
import jax.numpy as jnp
import numpy as np
from jax import lax
from jax.experimental import pallas as pl
from jax.experimental.pallas import tpu as pltpu

F32 = jnp.float32
BF16 = jnp.bfloat16
I32 = jnp.int32

D_MODEL = 1024
NSA_HEADS = 8
NSA_GROUPS = 2
NSA_REP = NSA_HEADS // NSA_GROUPS
HEAD_DIM = 64
NSA_WIDTH = NSA_HEADS * HEAD_DIM
KV_DIM = NSA_GROUPS * HEAD_DIM
NSA_BRANCHES = 3
CMP_STRIDE = 16
CMP_LEN = 32
CMP_HIDDEN = 256
SEL_BLOCK = 64
SEL_TOPN = 16
WINDOW = 512
HGRN_WIDTH = D_MODEL - NSA_WIDTH
HGRN_HEADS = 4
HGRN_DK = 128
HGRN_CHUNK = 64
HGRN_SUB = 8
MOE_GROUPS = 4
EXPERTS_PER_GROUP = 8
N_EXPERTS = MOE_GROUPS * EXPERTS_PER_GROUP
EXPERT_FF = 512
MOE_BLOCK = 256
RMS_EPS = 1e-6
NEG_INF = -1e30

LANES = 128
QKV_COLS = NSA_WIDTH + 6 * KV_DIM
HG_COLS = 4 * HGRN_WIDTH
IN_COLS_PAD = QKV_COLS + HG_COLS + LANES
GATE_ROWS = 32
VT_BLOCK = 128
LOG2E = 1.4426950408889634
VMEM_LIMIT = 56 * 1024 * 1024


def _dot(a, b):
    return jnp.dot(a, b, preferred_element_type=F32)


def _dot_nt(a, b):
    return lax.dot_general(a, b, (((1,), (1,)), ((), ())), preferred_element_type=F32)


def _dot_tn(a, b):
    return lax.dot_general(a, b, (((0,), (0,)), ((), ())), preferred_element_type=F32)


def _split3(x):
    a = x.astype(BF16)
    r = x - a.astype(F32)
    b = r.astype(BF16)
    c = (r - b.astype(F32)).astype(BF16)
    return a, b, c


def _silu(x):
    return x * jax.nn.sigmoid(x)


def _params(sem):
    return pltpu.CompilerParams(dimension_semantics=sem, vmem_limit_bytes=VMEM_LIMIT)


def _seg_rms(blk, low, gain):
    sq = blk * blk
    s_lo = jnp.sum(jnp.where(low, sq, 0.0), axis=-1, keepdims=True)
    s_hi = jnp.sum(jnp.where(low, 0.0, sq), axis=-1, keepdims=True)
    inv = jnp.where(low, lax.rsqrt(s_lo * (1.0 / HEAD_DIM) + RMS_EPS),
                    lax.rsqrt(s_hi * (1.0 / HEAD_DIM) + RMS_EPS))
    return blk * inv * gain


def _inproj_body(x_ref, g_ref, w_ref, gq_ref, gk_ref,
                 qt_ref, ksel_ref, vselt_ref, kwin_ref, vwint_ref, kc_ref, vc_ref,
                 gates_ref, hg_ref, *, tm, seq):
    x = x_ref[...]
    ms = jnp.mean(x * x, axis=-1, keepdims=True)
    n = (x * lax.rsqrt(ms + RMS_EPS) * g_ref[...]).astype(BF16)
    y = _dot(n, w_ref[...])

    lane = lax.broadcasted_iota(I32, (1, LANES), 1)
    low = lane < HEAD_DIM
    gq = gq_ref[...]
    scale = HEAD_DIM ** -0.5 * LOG2E
    for j in range(NSA_HEADS // 2):
        nrm_t = (_seg_rms(y[:, LANES * j:LANES * (j + 1)], low, gq) * scale).T.astype(BF16)
        qt_ref[2 * j] = nrm_t[0:HEAD_DIM]
        qt_ref[2 * j + 1] = nrm_t[HEAD_DIM:]

    def kv_block(idx):
        c0 = NSA_WIDTH + LANES * idx
        return y[:, c0:c0 + LANES]

    def split_groups(blk, hi_fill):
        g0 = jnp.where(low, blk, hi_fill)
        g1 = jnp.where(low, pltpu.roll(blk, HEAD_DIM, 1), hi_fill)
        return g0.astype(BF16), g1.astype(BF16)

    def store_transposed(vt_ref, blk):
        blk_t = blk.T.astype(BF16)
        for g in range(NSA_GROUPS):
            for c in range(tm // VT_BLOCK):
                vt_ref[g, c] = blk_t[g * HEAD_DIM:(g + 1) * HEAD_DIM, c * VT_BLOCK:(c + 1) * VT_BLOCK]

    kc_ref[...] = kv_block(0).astype(BF16)
    vc_ref[...] = kv_block(1).astype(BF16)

    row = lax.broadcasted_iota(I32, (tm, LANES), 0)
    t = (pl.program_id(0) * tm) % seq + row
    onehot = jnp.where(lane - HEAD_DIM == lax.shift_right_logical(t, 6), 1.0, 0.0)
    ks = _seg_rms(kv_block(2), low, gk_ref[1:2, :])
    ksel_ref[0], ksel_ref[1] = split_groups(ks, onehot)
    store_transposed(vselt_ref, kv_block(3))
    kw = _seg_rms(kv_block(4), low, gk_ref[2:3, :])
    kwin_ref[0], kwin_ref[1] = split_groups(kw, 0.0)
    store_transposed(vwint_ref, kv_block(5))

    hg_ref[...] = y[:, QKV_COLS:QKV_COLS + HG_COLS].astype(BF16)
    gates_ref[...] = y[:, QKV_COLS + HG_COLS:].T[0:GATE_ROWS]


def _inproj(x2, g_attn, w_in_p, gq128, gk128, *, seq, tm):
    n_tok = x2.shape[0]
    grid = (n_tok // tm,)
    row_spec = lambda cols: pl.BlockSpec((tm, cols), lambda i: (i, 0))
    k_spec = pl.BlockSpec((NSA_GROUPS, tm, LANES), lambda i: (0, i, 0))
    vt_spec = pl.BlockSpec((NSA_GROUPS, tm // VT_BLOCK, HEAD_DIM, VT_BLOCK), lambda i: (0, i, 0, 0))
    vt_shape = jax.ShapeDtypeStruct((NSA_GROUPS, n_tok // VT_BLOCK, HEAD_DIM, VT_BLOCK), BF16)
    full = lambda a: pl.BlockSpec(a.shape, lambda i: (0,) * a.ndim)
    out_shape = (
        jax.ShapeDtypeStruct((NSA_HEADS, HEAD_DIM, n_tok), BF16),
        jax.ShapeDtypeStruct((NSA_GROUPS, n_tok, LANES), BF16),
        vt_shape,
        jax.ShapeDtypeStruct((NSA_GROUPS, n_tok, LANES), BF16),
        vt_shape,
        jax.ShapeDtypeStruct((n_tok, LANES), BF16),
        jax.ShapeDtypeStruct((n_tok, LANES), BF16),
        jax.ShapeDtypeStruct((GATE_ROWS, n_tok), F32),
        jax.ShapeDtypeStruct((n_tok, HG_COLS), BF16),
    )
    out_specs = (pl.BlockSpec((NSA_HEADS, HEAD_DIM, tm), lambda i: (0, 0, i)),
                 k_spec, vt_spec, k_spec, vt_spec, row_spec(LANES), row_spec(LANES),
                 pl.BlockSpec((GATE_ROWS, tm), lambda i: (0, i)), row_spec(HG_COLS))
    return pl.pallas_call(
        functools.partial(_inproj_body, tm=tm, seq=seq),
        grid=grid,
        in_specs=[row_spec(D_MODEL), full(g_attn), full(w_in_p), full(gq128), full(gk128)],
        out_specs=out_specs,
        out_shape=out_shape,
        compiler_params=_params(("parallel",)),
        name="inproj",
    )(x2, g_attn, w_in_p, gq128, gk128)


def _compress_body(kc_ref, vc_ref, posk_ref, posv_ref, wk1_ref, wv1_ref, wk2_ref, wv2_ref,
                   gk_ref, kco_ref, vco_ref, *, nrow):
    lane = lax.broadcasted_iota(I32, (1, LANES), 1)
    low = lane < HEAD_DIM

    def mlp(x_ref, pos_ref, w1_ref, w2_ref):
        x = x_ref[0].astype(F32)
        ha = _dot((x + pos_ref[0:1, :]).astype(BF16), w1_ref[0])
        hb = _dot((x + pos_ref[1:2, :]).astype(BF16), w1_ref[1])
        hid = _silu(ha + pltpu.roll(hb, nrow - 1, 0))
        return _dot(hid.astype(BF16), w2_ref[...])

    def split_groups(blk):
        g0 = jnp.where(low, blk, 0.0)
        g1 = jnp.where(low, pltpu.roll(blk, HEAD_DIM, 1), 0.0)
        return g0.astype(BF16), g1.astype(BF16)

    kc = _seg_rms(mlp(kc_ref, posk_ref, wk1_ref, wk2_ref), low, gk_ref[0:1, :])
    kco_ref[0, 0], kco_ref[0, 1] = split_groups(kc)
    vc_t = mlp(vc_ref, posv_ref, wv1_ref, wv2_ref).T.astype(BF16)
    vco_ref[0, 0] = vc_t[0:HEAD_DIM]
    vco_ref[0, 1] = vc_t[HEAD_DIM:]


def _compress(kc3, vc3, posk, posv, wk1, wv1, wk2, wv2, gk128):
    batch, nrow, width = kc3.shape
    full = lambda a: pl.BlockSpec(a.shape, lambda b: (0,) * a.ndim)
    in_spec = pl.BlockSpec((1, nrow, width), lambda b: (b, 0, 0))
    k_spec = pl.BlockSpec((1, NSA_GROUPS, nrow, LANES), lambda b: (b, 0, 0, 0))
    vt_spec = pl.BlockSpec((1, NSA_GROUPS, HEAD_DIM, nrow), lambda b: (b, 0, 0, 0))
    return pl.pallas_call(
        functools.partial(_compress_body, nrow=nrow),
        grid=(batch,),
        in_specs=[in_spec, in_spec, full(posk), full(posv), full(wk1), full(wv1),
                  full(wk2), full(wv2), full(gk128)],
        out_specs=(k_spec, vt_spec),
        out_shape=(jax.ShapeDtypeStruct((batch, NSA_GROUPS, nrow, LANES), BF16),
                   jax.ShapeDtypeStruct((batch, NSA_GROUPS, HEAD_DIM, nrow), BF16)),
        compiler_params=_params(("parallel",)),
        name="compress",
    )(kc3, vc3, posk, posv, wk1, wv1, wk2, wv2, gk128)


SEL_CHUNK = 512


def _vt_cols(vt_ref, start, n):
    b0 = start // VT_BLOCK
    return jnp.concatenate([vt_ref[0, b0 + i] for i in range(n // VT_BLOCK)], axis=1)


def _attn_body(qt_ref, kc_ref, vct_ref, ks_ref, vst_ref, kw_ref, vwt_ref, gt_ref, o_ref,
               *, tq, seq, ncmp):
    g = pl.program_id(1)
    t0 = pl.program_id(2) * tq
    cols = NSA_REP * tq
    qt = jnp.concatenate([qt_ref[r] for r in range(NSA_REP)], axis=1)
    q_pad = jnp.concatenate([qt, jnp.zeros((HEAD_DIM, cols), BF16)], axis=0)
    tcol = t0 + (lax.broadcasted_iota(I32, (1, cols), 1) & (tq - 1))
    tlane = t0 + lax.broadcasted_iota(I32, (1, tq), 1)
    per_head = lambda a: jnp.concatenate([a] * NSA_REP, axis=1)

    s = _dot(kc_ref[0, 0], q_pad)
    cpos = lax.broadcasted_iota(I32, (ncmp, 1), 0) * CMP_STRIDE + (CMP_LEN - 1)
    cmask = cpos <= tcol
    s = jnp.where(cmask, s, NEG_INF)
    e = jnp.where(cmask, jnp.exp2(s - jnp.max(s, axis=0, keepdims=True)), 0.0)
    den = jnp.sum(e, axis=0, keepdims=True)
    p_c = e / jnp.where(den > 0.0, den, 1.0)
    o_c = _dot(vct_ref[0, 0], p_c.astype(BF16))

    p_sum = p_c[:, 0:tq]
    for r in range(1, NSA_REP):
        p_sum = p_sum + p_c[:, r * tq:(r + 1) * tq]
    nblk = seq // SEL_BLOCK
    jj = lax.broadcasted_iota(I32, (nblk, ncmp), 0) * SEL_BLOCK
    cc = lax.broadcasted_iota(I32, (nblk, ncmp), 1) * CMP_STRIDE
    ov = jnp.maximum(jnp.minimum(cc + CMP_LEN, jj + SEL_BLOCK) - jnp.maximum(cc, jj), 0)
    ov = (ov.astype(F32) * (1.0 / CMP_LEN)).astype(BF16)
    p1, p2, p3 = _split3(p_sum)
    imp = _dot(ov, p1) + _dot(ov, p2) + _dot(ov, p3)

    jb = lax.broadcasted_iota(I32, (nblk, tq), 0)
    cur = lax.shift_right_logical(t0 + lax.broadcasted_iota(I32, (nblk, tq), 1), 6)
    valid = jb <= cur
    forced = (jb == cur) | (jb == 0)
    imp = jnp.where(forced, jnp.inf, jnp.where(valid, imp, -jnp.inf))
    rank = jnp.zeros((nblk, tq), F32)
    for j2 in range(nblk):
        other = imp[j2:j2 + 1, :]
        ahead = (other > imp) | ((other == imp) & (jb > j2))
        rank = rank + jnp.where(ahead, 1.0, 0.0)
    sel = (rank < float(min(SEL_TOPN, nblk))) & valid
    bias = jnp.where(sel, 0.0, NEG_INF).astype(BF16)
    if nblk < HEAD_DIM:
        bias = jnp.concatenate([bias, jnp.zeros((HEAD_DIM - nblk, tq), BF16)], axis=0)
    q_sel = jnp.concatenate([qt, per_head(bias)], axis=0)

    def sel_step(start, extra, carry):
        m_old, l_old, acc = carry
        sc = _dot(ks_ref[0, pl.ds(start, SEL_CHUNK), :], q_sel)
        if extra is not None:
            sc = sc + extra
        m_new = jnp.maximum(m_old, jnp.max(sc, axis=0, keepdims=True))
        alpha = jnp.exp2(m_old - m_new)
        p = jnp.exp2(sc - m_new)
        l_new = alpha * l_old + jnp.sum(p, axis=0, keepdims=True)
        acc = alpha * acc + _dot(_vt_cols(vst_ref, start, SEL_CHUNK), p.astype(BF16))
        return m_new, l_new, acc

    n_full = t0 // SEL_CHUNK
    init = (jnp.full((1, cols), -3.0e38, F32), jnp.zeros((1, cols), F32), jnp.zeros((HEAD_DIM, cols), F32))
    carry = lax.fori_loop(
        0, n_full, lambda ci, c: sel_step(pl.multiple_of(ci * SEL_CHUNK, SEL_CHUNK), None, c), init)
    tail0 = pl.multiple_of(n_full * SEL_CHUNK, SEL_CHUNK)
    kpos = tail0 + lax.broadcasted_iota(I32, (SEL_CHUNK, 1), 0)
    causal = jnp.where(kpos <= tlane, 0.0, NEG_INF)
    _, l_s, acc_s = sel_step(tail0, per_head(causal), carry)
    o_s = acc_s / l_s

    band = WINDOW + tq
    w0 = pl.multiple_of(jnp.maximum(t0 - WINDOW, 0), tq)
    sw = _dot(kw_ref[0, pl.ds(w0, band), :], q_pad)
    dist = tlane - (w0 + lax.broadcasted_iota(I32, (band, 1), 0))
    in_win = lax.bitcast_convert_type(dist, jnp.uint32) < jnp.uint32(WINDOW)
    sw = sw + per_head(jnp.where(in_win, 0.0, NEG_INF))
    pw = jnp.exp2(sw - jnp.max(sw, axis=0, keepdims=True))
    o_w = _dot(_vt_cols(vwt_ref, w0, band), pw.astype(BF16)) / jnp.sum(pw, axis=0, keepdims=True)

    def gate(r, br):
        col = (g * NSA_REP + r) * NSA_BRANCHES + br
        return jax.nn.sigmoid(gt_ref[pl.ds(col, 1), :])

    heads = []
    for r in range(NSA_REP):
        sl = slice(r * tq, (r + 1) * tq)
        heads.append(gate(r, 0) * o_c[:, sl] + gate(r, 1) * o_s[:, sl] + gate(r, 2) * o_w[:, sl])
    o_ref[...] = jnp.concatenate(heads, axis=0).T.astype(BF16)


def _attention(qt, kcmp, vcmpt, ksel, vselt, kwin, vwint, gates_t, *, batch, seq, tq):
    n_tok = batch * seq
    ncmp = kcmp.shape[2]
    nq = seq // tq
    nvb = seq // VT_BLOCK
    k_spec = pl.BlockSpec((1, seq, LANES), lambda b, g, i: (g, b, 0))
    vt_spec = pl.BlockSpec((1, nvb, HEAD_DIM, VT_BLOCK), lambda b, g, i: (g, b, 0, 0))
    return pl.pallas_call(
        functools.partial(_attn_body, tq=tq, seq=seq, ncmp=ncmp),
        grid=(batch, NSA_GROUPS, nq),
        in_specs=[
            pl.BlockSpec((NSA_REP, HEAD_DIM, tq), lambda b, g, i: (g, 0, b * nq + i)),
            pl.BlockSpec((1, 1, ncmp, LANES), lambda b, g, i: (b, g, 0, 0)),
            pl.BlockSpec((1, 1, HEAD_DIM, ncmp), lambda b, g, i: (b, g, 0, 0)),
            k_spec, vt_spec, k_spec, vt_spec,
            pl.BlockSpec((GATE_ROWS, tq), lambda b, g, i: (0, b * nq + i)),
        ],
        out_specs=pl.BlockSpec((tq, NSA_REP * HEAD_DIM), lambda b, g, i: (b * nq + i, g)),
        out_shape=jax.ShapeDtypeStruct((n_tok, NSA_WIDTH), BF16),
        compiler_params=_params(("parallel", "parallel", "arbitrary")),
        name="nsa_attention",
    )(qt, kcmp, vcmpt, ksel, vselt, kwin, vwint, gates_t)


def _hgrn_body(q_ref, f_ref, i_ref, g_ref, lb_ref, gn_ref, o_ref, st_ref, *, tc):
    C, SUB = HGRN_CHUNK, HGRN_SUB

    @pl.when(pl.program_id(2) == 0)
    def _():
        st_ref[...] = jnp.zeros_like(st_ref)

    r = lax.broadcasted_iota(I32, (C, C), 0)
    c = lax.broadcasted_iota(I32, (C, C), 1)
    assert C // SUB == 8, "the cross-sub-chunk factorisation below is written for 8 sub-chunks"
    sub_shift = SUB.bit_length() - 1
    rs, cs = r >> sub_shift, c >> sub_shift
    one = lambda m: jnp.where(m, 1.0, 0.0).astype(BF16)
    sum_mats = jnp.concatenate([one(c <= r), one((c <= r) & (rs == cs)),
                                one((c > r) & (rs == cs)), one(c > r)], axis=0)
    row = lax.broadcasted_iota(I32, (C, 1), 0)
    rsub = row >> sub_shift
    lb = lb_ref[...]
    gn = gn_ref[...]

    def chunk(ci, carry):
        sl = pl.ds(pl.multiple_of(ci * C, C), C)
        q = _silu(q_ref[sl, :].astype(F32))
        fg = lb + (1.0 - lb) * jax.nn.sigmoid(f_ref[sl, :].astype(F32))
        k = 1.0 - fg
        v = i_ref[sl, :].astype(F32)
        l1, l2, l3 = _split3(jnp.log(fg))
        sums = _dot(sum_mats, l1) + _dot(sum_mats, l2) + _dot(sum_mats, l3)
        a_full = sums[0:C]
        a_sub = sums[C:2 * C]
        b_sub = sums[2 * C:3 * C]
        b_full = sums[3 * C:4 * C]

        out = jnp.sum(q * k, axis=-1, keepdims=True) * v
        for d in range(1, SUB):
            ok = (row & (SUB - 1)) >= d
            arg = jnp.where(ok, a_sub - pltpu.roll(a_sub, d, 0), NEG_INF)
            w = jnp.sum(q * pltpu.roll(k, d, 0) * jnp.exp(arg), axis=-1, keepdims=True)
            out = out + w * pltpu.roll(v, d, 0)

        qd = q * jnp.exp(a_sub)
        kd = k * jnp.exp(b_sub)
        tot = [a_sub[(m + 1) * SUB - 1:(m + 1) * SUB, :] for m in range(C // SUB)]

        def side(terms):
            f = jnp.ones((C, HGRN_DK), F32)
            for sub, ms in terms.items():
                f = jnp.where(rsub == sub, jnp.exp(sum(tot[m] for m in ms)), f)
            return f

        a_adj = _dot_nt(qd.astype(BF16), kd.astype(BF16))
        q_half = qd * side({3: [2], 7: [6]})
        k_half = kd * side({0: [1], 4: [5]})
        a_half = _dot_nt(q_half.astype(BF16), k_half.astype(BF16))
        q_mid = qd * side({5: [4], 6: [4, 5], 7: [4, 5, 6]})
        k_mid = kd * side({2: [3], 1: [2, 3], 0: [1, 2, 3]})
        a_mid = _dot_nt(q_mid.astype(BF16), k_mid.astype(BF16))
        far = rs - cs >= 2
        a_off = jnp.where(rs == cs + 1, a_adj,
                          jnp.where(far & ((rs >> 2) == (cs >> 2)), a_half, jnp.where(far, a_mid, 0.0)))
        vb = v.astype(BF16)
        out = out + _dot(a_off.astype(BF16), vb)

        st = st_ref[...]
        out = out + _dot_nt((q * jnp.exp(a_full)).astype(BF16), st.astype(BF16))
        k_dec = (k * jnp.exp(b_full)).astype(BF16)
        st_ref[...] = st * jnp.exp(a_full[C - 1:C, :]) + _dot_tn(vb, k_dec)

        ms = jnp.mean(out * out, axis=-1, keepdims=True)
        on = out * lax.rsqrt(ms + RMS_EPS) * gn
        o_ref[sl, :] = (on * _silu(g_ref[sl, :].astype(F32))).astype(BF16)
        return carry

    lax.fori_loop(0, tc // C, chunk, 0, unroll=4)


def _hgrn(hg, lb, gn, *, batch, seq, tc):
    n_tok = batch * seq
    nt = seq // tc
    part = lambda p: pl.BlockSpec((tc, HGRN_DK), lambda b, h, i, p=p: (b * nt + i, p * HGRN_HEADS + h))
    return pl.pallas_call(
        functools.partial(_hgrn_body, tc=tc),
        grid=(batch, HGRN_HEADS, nt),
        in_specs=[part(0), part(1), part(2), part(3),
                  pl.BlockSpec((1, HGRN_DK), lambda b, h, i: (0, h)),
                  pl.BlockSpec((1, HGRN_DK), lambda b, h, i: (0, 0))],
        out_specs=pl.BlockSpec((tc, HGRN_DK), lambda b, h, i: (b * nt + i, h)),
        out_shape=jax.ShapeDtypeStruct((n_tok, HGRN_WIDTH), BF16),
        scratch_shapes=[pltpu.VMEM((HGRN_DK, HGRN_DK), F32)],
        compiler_params=_params(("parallel", "parallel", "arbitrary")),
        name="hgrn2",
    )(hg, hg, hg, hg, lb, gn)


def _outproj_body(on_ref, oh_ref, x_ref, gno_ref, wo1_ref, wo2_ref, gf_ref, wr_ref, br_ref,
                  h_ref, hn_ref, route_ref):
    a = on_ref[...].astype(F32)
    ms = jnp.mean(a * a, axis=-1, keepdims=True)
    an = (a * lax.rsqrt(ms + RMS_EPS) * gno_ref[...]).astype(BF16)
    h = x_ref[...] + _dot(an, wo1_ref[...]) + _dot(oh_ref[...], wo2_ref[...])
    h_ref[...] = h
    ms2 = jnp.mean(h * h, axis=-1, keepdims=True)
    hn = h * lax.rsqrt(ms2 + RMS_EPS) * gf_ref[...]
    tm = hn.shape[0]
    for s in range(D_MODEL // LANES):
        hn_ref[pl.ds(s, tm, stride=D_MODEL // LANES), :] = hn[:, s * LANES:(s + 1) * LANES]

    hn_hi = hn.astype(BF16)
    hn_lo = (hn - hn_hi.astype(F32)).astype(BF16)
    lg = (_dot(hn_hi, wr_ref[0]) + _dot(hn_lo, wr_ref[0]) + _dot(hn_hi, wr_ref[1])) + br_ref[...]

    lane = lax.broadcasted_iota(I32, (1, LANES), 1).astype(F32)
    first = lambda hit: jnp.min(jnp.where(hit, lane, 1e9), axis=-1, keepdims=True)
    gmask = lane < MOE_GROUPS
    lgm = jnp.where(gmask, lg, -jnp.inf)
    mg = jnp.max(lgm, axis=-1, keepdims=True)
    pg_top = 1.0 / jnp.sum(jnp.where(gmask, jnp.exp(lgm - mg), 0.0), axis=-1, keepdims=True)
    gidx = first(lgm == mg)
    eloc = lane - MOE_GROUPS
    emask = (eloc >= 0) & (eloc < N_EXPERTS) & (jnp.floor(eloc * (1.0 / EXPERTS_PER_GROUP)) == gidx)
    le1 = jnp.where(emask, lg, -jnp.inf)
    m1 = jnp.max(le1, axis=-1, keepdims=True)
    i1 = first(le1 == m1)
    le2 = jnp.where(lane == i1, -jnp.inf, le1)
    m2 = jnp.max(le2, axis=-1, keepdims=True)
    i2 = first(le2 == m2)
    e2 = jnp.exp(m2 - m1)
    w1 = pg_top / (1.0 + e2)
    w2 = pg_top * e2 / (1.0 + e2)
    route_ref[...] = jnp.where(lane == 0, i1 - MOE_GROUPS,
                     jnp.where(lane == 1, i2 - MOE_GROUPS,
                     jnp.where(lane == 2, w1, jnp.where(lane == 3, w2, 0.0))))


def _outproj(o_nsa, o_hgrn, x2, gno, wo1, wo2, gf, wr, br, *, tm):
    n_tok = x2.shape[0]
    row_spec = lambda cols: pl.BlockSpec((tm, cols), lambda i: (i, 0))
    full = lambda a: pl.BlockSpec(a.shape, lambda i: (0,) * a.ndim)
    return pl.pallas_call(
        _outproj_body,
        grid=(n_tok // tm,),
        in_specs=[row_spec(NSA_WIDTH), row_spec(HGRN_WIDTH), row_spec(D_MODEL), full(gno),
                  full(wo1), full(wo2), full(gf), full(wr), full(br)],
        out_specs=(row_spec(D_MODEL),
                   pl.BlockSpec((tm * (D_MODEL // LANES), LANES), lambda i: (i, 0)),
                   row_spec(LANES)),
        out_shape=(jax.ShapeDtypeStruct((n_tok, D_MODEL), F32),
                   jax.ShapeDtypeStruct((n_tok * (D_MODEL // LANES), LANES), F32),
                   jax.ShapeDtypeStruct((n_tok, LANES), F32)),
        compiler_params=_params(("parallel",)),
        name="outproj_router",
    )(o_nsa, o_hgrn, x2, gno, wo1, wo2, gf, wr, br)


SLAB = D_MODEL // LANES
ROW_BITS = 17


def _expert_body(be_ref, nb_ref, src_ref, hn_ref, w1_ref, w3_ref, w2_ref, y_ref,
                 xbuf, ybuf, gsem, ssem, *, n_asg):
    i = pl.program_id(0)
    n_used = nb_ref[0]
    R = MOE_BLOCK
    last_blk = pl.num_programs(0) - 1

    def slab(r):
        return pl.ds(r * SLAB if isinstance(r, int) else pl.multiple_of(r * SLAB, SLAB), SLAB)

    def gather_row(blk, slot, r):
        tok = lax.shift_right_logical(src_ref[blk * R + r], ROW_BITS)
        pltpu.make_async_copy(hn_ref.at[tok], xbuf.at[slot, slab(r), :], gsem.at[slot]).start()

    def gather_wait(slot):
        pltpu.make_async_copy(xbuf.at[slot], xbuf.at[slot], gsem.at[slot]).wait()

    def scatter_slab(r, row):
        pltpu.make_async_copy(ybuf.at[slab(r), :], y_ref.at[row], ssem).start()

    def scatter_row(blk, r):
        scatter_slab(r, src_ref[blk * R + r] & ((1 << ROW_BITS) - 1))

    def scatter_wait():
        pltpu.make_async_copy(ybuf, ybuf, ssem).wait()

    def rolled(fn):
        def body(r, c):
            fn(r)
            return c
        lax.fori_loop(0, R, body, 0, unroll=8)

    @pl.when(i == 0)
    def _():
        ybuf[...] = jnp.zeros_like(ybuf)
        rolled(lambda r: scatter_slab(r, n_asg + r))
        scatter_wait()
        rolled(lambda r: gather_row(0, 0, r))

    @pl.when(i < n_used)
    def _():
        slot = i & 1
        gather_wait(slot)
        x = jnp.concatenate([xbuf[slot, pl.ds(s, R, stride=SLAB), :] for s in range(SLAB)], axis=1).astype(BF16)
        nxt = jnp.minimum(i + 1, n_used - 1)
        prev = jnp.where(i > 0, i - 1, last_blk)
        for r in range(R):
            gather_row(nxt, 1 - slot, r)
            scatter_row(prev, r)
        hid = _silu(_dot(x, w1_ref[0])) * _dot(x, w3_ref[0])
        y = _dot(hid.astype(BF16), w2_ref[0])
        scatter_wait()
        for s in range(SLAB):
            ybuf[pl.ds(s, R, stride=SLAB), :] = y[:, s * LANES:(s + 1) * LANES]

        @pl.when(i == n_used - 1)
        def _():
            rolled(lambda r: scatter_row(i, r))
            scatter_wait()
            gather_wait(1 - slot)


def _experts(blk_e, n_used, row_map, hn_slabs, w1, w3, w2):
    p_rows = row_map.shape[0]
    n_asg = 2 * hn_slabs.shape[0]
    nblk = p_rows // MOE_BLOCK
    wspec = lambda a: pl.BlockSpec((1,) + a.shape[1:], lambda i, be, nb, src: (be[i], 0, 0))
    grid_spec = pltpu.PrefetchScalarGridSpec(
        num_scalar_prefetch=3,
        grid=(nblk,),
        in_specs=[pl.BlockSpec(memory_space=pl.ANY), wspec(w1), wspec(w3), wspec(w2)],
        out_specs=pl.BlockSpec(memory_space=pl.ANY),
        scratch_shapes=[pltpu.VMEM((2, MOE_BLOCK * SLAB, LANES), F32), pltpu.VMEM((MOE_BLOCK * SLAB, LANES), F32),
                        pltpu.SemaphoreType.DMA((2,)), pltpu.SemaphoreType.DMA],
    )
    return pl.pallas_call(
        functools.partial(_expert_body, n_asg=n_asg),
        grid_spec=grid_spec,
        out_shape=jax.ShapeDtypeStruct((n_asg + MOE_BLOCK, SLAB, LANES), F32),
        compiler_params=_params(("arbitrary",)),
        name="experts",
    )(blk_e, n_used, row_map, hn_slabs, w1, w3, w2)


def _combine_body(h_ref, y0_ref, y1_ref, rt_ref, o_ref, *, tm):
    rt = rt_ref[...]
    w0, w1 = rt[:, 2:3], rt[:, 3:4]
    for s in range(SLAB):
        cols = slice(s * LANES, (s + 1) * LANES)
        y0 = y0_ref[pl.ds(s, tm, stride=SLAB), :]
        y1 = y1_ref[pl.ds(s, tm, stride=SLAB), :]
        o_ref[:, cols] = h_ref[:, cols] + (w0 * y0 + w1 * y1)


def _combine(h, y_slabs, route, *, tm):
    n_tok = h.shape[0]
    nb = n_tok // tm
    y2d = y_slabs.reshape(-1, LANES)
    return pl.pallas_call(
        functools.partial(_combine_body, tm=tm),
        grid=(nb,),
        in_specs=[pl.BlockSpec((tm, D_MODEL), lambda i: (i, 0)),
                  pl.BlockSpec((tm * SLAB, LANES), lambda i: (i, 0)),
                  pl.BlockSpec((tm * SLAB, LANES), lambda i: (nb + i, 0)),
                  pl.BlockSpec((tm, LANES), lambda i: (i, 0))],
        out_specs=pl.BlockSpec((tm, D_MODEL), lambda i: (i, 0)),
        out_shape=jax.ShapeDtypeStruct((n_tok, D_MODEL), F32),
        compiler_params=_params(("parallel",)),
        name="moe_combine",
    )(h, y2d, y2d, route)


def _pair_groups(w):
    rows, cols = w.shape
    z = jnp.zeros((NSA_GROUPS, rows, NSA_GROUPS, cols), w.dtype)
    for g in range(NSA_GROUPS):
        z = z.at[g, :, g, :].set(w)
    return z


def _compress_weights(w1, w2, pos):
    w1r = w1.reshape(2, CMP_STRIDE, HEAD_DIM, CMP_HIDDEN)
    halves = []
    for half in range(2):
        z = jnp.zeros((CMP_STRIDE, NSA_GROUPS, HEAD_DIM, NSA_GROUPS, CMP_HIDDEN), F32)
        for g in range(NSA_GROUPS):
            z = z.at[:, g, :, g, :].set(w1r[half])
        halves.append(z.reshape(CMP_STRIDE * KV_DIM, NSA_GROUPS * CMP_HIDDEN))
    w1p = jnp.stack(halves).astype(BF16)
    w2p = _pair_groups(w2).reshape(NSA_GROUPS * CMP_HIDDEN, KV_DIM).astype(BF16)
    posr = pos.reshape(2, CMP_STRIDE, 1, HEAD_DIM)
    posp = jnp.broadcast_to(posr, (2, CMP_STRIDE, NSA_GROUPS, HEAD_DIM)).reshape(2, CMP_STRIDE * KV_DIM)
    return w1p, w2p, posp


def kernel(x, attn_norm_g, ffn_norm_g, w_in, w_out, nsa_q_norm_g, nsa_k_norm_g, cmp_pos_k, cmp_pos_v,
           cmp_wk1, cmp_wk2, cmp_wv1, cmp_wv2, nsa_out_norm_g, hgrn_lb_logits, hgrn_out_norm_g,
           moe_w_group, moe_b_group, moe_w_expert, moe_b_expert, moe_w1, moe_w3, moe_w2):
    batch, seq, _ = x.shape
    n_tok = batch * seq
    assert seq % SEL_CHUNK == 0 and seq >= WINDOW + 128 and seq // SEL_BLOCK <= HEAD_DIM
    depth = w_in.shape[0]
    lb_all = jnp.cumsum(jax.nn.softmax(hgrn_lb_logits.astype(F32), axis=0), axis=0)
    tm = min(512, n_tok)
    h = x.reshape(n_tok, D_MODEL)
    for layer in range(depth):
        wl = w_in[layer]
        gate_cols = wl[:, QKV_COLS:QKV_COLS + NSA_HEADS * NSA_BRANCHES]
        w_in_p = jnp.concatenate(
            [wl[:, :QKV_COLS], wl[:, QKV_COLS + NSA_HEADS * NSA_BRANCHES:],
             jnp.pad(gate_cols, ((0, 0), (0, LANES - NSA_HEADS * NSA_BRANCHES)))], axis=1).astype(BF16)
        gq128 = jnp.tile(nsa_q_norm_g[layer], 2)[None, :]
        gk128 = jnp.tile(nsa_k_norm_g[layer], (1, 2))
        g_attn = attn_norm_g[layer][None, :]

        qt, ksel, vselt, kwin, vwint, kc_raw, vc_raw, gates_t, hg = _inproj(
            h, g_attn, w_in_p, gq128, gk128, seq=seq, tm=tm)

        wk1p, wk2p, poskp = _compress_weights(cmp_wk1[layer], cmp_wk2[layer], cmp_pos_k[layer])
        wv1p, wv2p, posvp = _compress_weights(cmp_wv1[layer], cmp_wv2[layer], cmp_pos_v[layer])
        nrow = seq // CMP_STRIDE
        kcmp, vcmpt = _compress(kc_raw.reshape(batch, nrow, CMP_STRIDE * KV_DIM),
                                vc_raw.reshape(batch, nrow, CMP_STRIDE * KV_DIM),
                                poskp, posvp, wk1p, wv1p, wk2p, wv2p, gk128)

        o_nsa = _attention(qt, kcmp, vcmpt, ksel, vselt, kwin, vwint, gates_t,
                           batch=batch, seq=seq, tq=256)
        o_hgrn = _hgrn(hg, lb_all[layer][None, :], hgrn_out_norm_g[layer][None, :],
                       batch=batch, seq=seq, tc=min(512, seq))

        wo = w_out[layer].astype(BF16)
        wr = jnp.concatenate([moe_w_group[layer], moe_w_expert[layer]], axis=1)
        wr = jnp.pad(wr, ((0, 0), (0, LANES - wr.shape[1])))
        wr_hi = wr.astype(BF16)
        wr_lo = (wr - wr_hi.astype(F32)).astype(BF16)
        br = jnp.pad(jnp.concatenate([moe_b_group[layer], moe_b_expert[layer]]),
                     (0, LANES - MOE_GROUPS - N_EXPERTS))[None, :]
        h, hn, route = _outproj(o_nsa, o_hgrn, h, nsa_out_norm_g[layer][None, :],
                                wo[:NSA_WIDTH], wo[NSA_WIDTH:], ffn_norm_g[layer][None, :],
                                jnp.stack([wr_hi, wr_lo]), br, tm=tm)

        e_flat = route[:, 0:2].astype(I32).reshape(-1)
        n_asg = e_flat.shape[0]
        assert 2 * n_tok + MOE_BLOCK <= 1 << ROW_BITS and n_tok <= 1 << (32 - ROW_BITS)
        order = jnp.argsort(e_flat, stable=True).astype(I32)
        counts = jnp.sum((e_flat[:, None] == jnp.arange(N_EXPERTS)[None, :]).astype(I32), axis=0)
        start = jnp.cumsum(counts) - counts
        padded = (counts + MOE_BLOCK - 1) // MOE_BLOCK * MOE_BLOCK
        pend = jnp.cumsum(padded)
        p_rows = n_asg + N_EXPERTS * MOE_BLOCK
        nblk = p_rows // MOE_BLOCK
        blk_e = jnp.minimum(jnp.sum(jnp.arange(nblk)[:, None] * MOE_BLOCK >= pend[None, :], axis=1),
                            N_EXPERTS - 1).astype(I32)
        n_used = (pend[-1] // MOE_BLOCK).astype(I32)[None]
        j_in = jnp.arange(MOE_BLOCK, dtype=I32)[None, :]
        r0 = jnp.arange(nblk, dtype=I32) * MOE_BLOCK - (pend - padded)[blk_e]
        live = (r0[:, None] + j_in) < counts[blk_e][:, None]
        order_pad = jnp.concatenate([order, jnp.zeros((MOE_BLOCK,), I32)])
        run0 = jnp.clip(start[blk_e] + r0, 0, n_asg)
        asg = jax.vmap(lambda o: lax.dynamic_slice(order_pad, (o,), (MOE_BLOCK,)))(run0)
        tok = asg >> 1
        out_row = jnp.where(live, (asg & 1) * n_tok + tok, n_asg + j_in)
        row_map = ((jnp.where(live, tok, 0) << ROW_BITS) | out_row).reshape(p_rows)
        y_slabs = _experts(blk_e, n_used, row_map, hn.reshape(n_tok, SLAB, LANES),
                           moe_w1[layer].astype(BF16), moe_w3[layer].astype(BF16), moe_w2[layer].astype(BF16))
        h = _combine(h, y_slabs, route, tm=min(256, n_tok))
    return h.reshape(batch, seq, D_MODEL)
```

```python
import functools

import jax
import jax.numpy as jnp
import numpy as np
from jax import lax
from jax.experimental import pallas as pl
from jax.experimental.pallas import tpu as pltpu

F32 = jnp.float32
BF16 = jnp.bfloat16
I32 = jnp.int32

D_MODEL = 1024
NSA_HEADS = 8
NSA_GROUPS = 2
NSA_REP = NSA_HEADS // NSA_GROUPS
HEAD_DIM = 64
NSA_WIDTH = NSA_HEADS * HEAD_DIM
KV_DIM = NSA_GROUPS * HEAD_DIM
NSA_BRANCHES = 3
CMP_STRIDE = 16
CMP_LEN = 32
CMP_HIDDEN = 256
SEL_BLOCK = 64
SEL_TOPN = 16
WINDOW = 512
HGRN_WIDTH = D_MODEL - NSA_WIDTH
HGRN_HEADS = 4
HGRN_DK = 128
HGRN_CHUNK = 64
HGRN_SUB = 8
MOE_GROUPS = 4
EXPERTS_PER_GROUP = 8
N_EXPERTS = MOE_GROUPS * EXPERTS_PER_GROUP
EXPERT_FF = 512
MOE_BLOCK = 256
RMS_EPS = 1e-6
NEG_INF = -1e30

LANES = 128
QKV_COLS = NSA_WIDTH + 6 * KV_DIM
HG_COLS = 4 * HGRN_WIDTH
IN_COLS_PAD = QKV_COLS + HG_COLS + LANES
GATE_ROWS = 32
VT_BLOCK = 128
LOG2E = 1.4426950408889634
VMEM_LIMIT = 56 * 1024 * 1024


def _dot(a, b):
    return jnp.dot(a, b, preferred_element_type=F32)


def _dot_nt(a, b):
    return lax.dot_general(a, b, (((1,), (1,)), ((), ())), preferred_element_type=F32)


def _dot_tn(a, b):
    return lax.dot_general(a, b, (((0,), (0,)), ((), ())), preferred_element_type=F32)


def _split3(x):
    a = x.astype(BF16)
    r = x - a.astype(F32)
    b = r.astype(BF16)
    c = (r - b.astype(F32)).astype(BF16)
    return a, b, c


def _silu(x):
    return x * jax.nn.sigmoid(x)


def _params(sem):
    return pltpu.CompilerParams(dimension_semantics=sem, vmem_limit_bytes=VMEM_LIMIT)


def _seg_rms(blk, low, gain):
    sq = blk * blk
    s_lo = jnp.sum(jnp.where(low, sq, 0.0), axis=-1, keepdims=True)
    s_hi = jnp.sum(jnp.where(low, 0.0, sq), axis=-1, keepdims=True)
    inv = jnp.where(low, lax.rsqrt(s_lo * (1.0 / HEAD_DIM) + RMS_EPS),
                    lax.rsqrt(s_hi * (1.0 / HEAD_DIM) + RMS_EPS))
    return blk * inv * gain


def _inproj_body(x_ref, g_ref, w_ref, gq_ref, gk_ref,
                 qt_ref, ksel_ref, vselt_ref, kwin_ref, vwint_ref, kc_ref, vc_ref,
                 gates_ref, hg_ref, *, tm, seq):
    x = x_ref[...]
    ms = jnp.mean(x * x, axis=-1, keepdims=True)
    n = (x * lax.rsqrt(ms + RMS_EPS) * g_ref[...]).astype(BF16)
    y = _dot(n, w_ref[...])

    lane = lax.broadcasted_iota(I32, (1, LANES), 1)
    low = lane < HEAD_DIM
    gq = gq_ref[...]
    scale = HEAD_DIM ** -0.5 * LOG2E
    for j in range(NSA_HEADS // 2):
        nrm_t = (_seg_rms(y[:, LANES * j:LANES * (j + 1)], low, gq) * scale).T.astype(BF16)
        qt_ref[2 * j] = nrm_t[0:HEAD_DIM]
        qt_ref[2 * j + 1] = nrm_t[HEAD_DIM:]

    def kv_block(idx):
        c0 = NSA_WIDTH + LANES * idx
        return y[:, c0:c0 + LANES]

    def split_groups(blk, hi_fill):
        g0 = jnp.where(low, blk, hi_fill)
        g1 = jnp.where(low, pltpu.roll(blk, HEAD_DIM, 1), hi_fill)
        return g0.astype(BF16), g1.astype(BF16)

    def store_transposed(vt_ref, blk):
        blk_t = blk.T.astype(BF16)
        for g in range(NSA_GROUPS):
            for c in range(tm // VT_BLOCK):
                vt_ref[g, c] = blk_t[g * HEAD_DIM:(g + 1) * HEAD_DIM, c * VT_BLOCK:(c + 1) * VT_BLOCK]

    kc_ref[...] = kv_block(0).astype(BF16)
    vc_ref[...] = kv_block(1).astype(BF16)

    row = lax.broadcasted_iota(I32, (tm, LANES), 0)
    t = (pl.program_id(0) * tm) % seq + row
    onehot = jnp.where(lane - HEAD_DIM == lax.shift_right_logical(t, 6), 1.0, 0.0)
    ks = _seg_rms(kv_block(2), low, gk_ref[1:2, :])
    ksel_ref[0], ksel_ref[1] = split_groups(ks, onehot)
    store_transposed(vselt_ref, kv_block(3))
    kw = _seg_rms(kv_block(4), low, gk_ref[2:3, :])
    kwin_ref[0], kwin_ref[1] = split_groups(kw, 0.0)
    store_transposed(vwint_ref, kv_block(5))

    hg_ref[...] = y[:, QKV_COLS:QKV_COLS + HG_COLS].astype(BF16)
    gates_ref[...] = y[:, QKV_COLS + HG_COLS:].T[0:GATE_ROWS]


def _inproj(x2, g_attn, w_in_p, gq128, gk128, *, seq, tm):
    n_tok = x2.shape[0]
    grid = (n_tok // tm,)
    row_spec = lambda cols: pl.BlockSpec((tm, cols), lambda i: (i, 0))
    k_spec = pl.BlockSpec((NSA_GROUPS, tm, LANES), lambda i: (0, i, 0))
    vt_spec = pl.BlockSpec((NSA_GROUPS, tm // VT_BLOCK, HEAD_DIM, VT_BLOCK), lambda i: (0, i, 0, 0))
    vt_shape = jax.ShapeDtypeStruct((NSA_GROUPS, n_tok // VT_BLOCK, HEAD_DIM, VT_BLOCK), BF16)
    full = lambda a: pl.BlockSpec(a.shape, lambda i: (0,) * a.ndim)
    out_shape = (
        jax.ShapeDtypeStruct((NSA_HEADS, HEAD_DIM, n_tok), BF16),
        jax.ShapeDtypeStruct((NSA_GROUPS, n_tok, LANES), BF16),
        vt_shape,
        jax.ShapeDtypeStruct((NSA_GROUPS, n_tok, LANES), BF16),
        vt_shape,
        jax.ShapeDtypeStruct((n_tok, LANES), BF16),
        jax.ShapeDtypeStruct((n_tok, LANES), BF16),
        jax.ShapeDtypeStruct((GATE_ROWS, n_tok), F32),
        jax.ShapeDtypeStruct((n_tok, HG_COLS), BF16),
    )
    out_specs = (pl.BlockSpec((NSA_HEADS, HEAD_DIM, tm), lambda i: (0, 0, i)),
                 k_spec, vt_spec, k_spec, vt_spec, row_spec(LANES), row_spec(LANES),
                 pl.BlockSpec((GATE_ROWS, tm), lambda i: (0, i)), row_spec(HG_COLS))
    return pl.pallas_call(
        functools.partial(_inproj_body, tm=tm, seq=seq),
        grid=grid,
        in_specs=[row_spec(D_MODEL), full(g_attn), full(w_in_p), full(gq128), full(gk128)],
        out_specs=out_specs,
        out_shape=out_shape,
        compiler_params=_params(("parallel",)),
        name="inproj",
    )(x2, g_attn, w_in_p, gq128, gk128)


def _compress_body(kc_ref, vc_ref, posk_ref, posv_ref, wk1_ref, wv1_ref, wk2_ref, wv2_ref,
                   gk_ref, kco_ref, vco_ref, *, nrow):
    lane = lax.broadcasted_iota(I32, (1, LANES), 1)
    low = lane < HEAD_DIM

    def mlp(x_ref, pos_ref, w1_ref, w2_ref):
        x = x_ref[0].astype(F32)
        ha = _dot((x + pos_ref[0:1, :]).astype(BF16), w1_ref[0])
        hb = _dot((x + pos_ref[1:2, :]).astype(BF16), w1_ref[1])
        hid = _silu(ha + pltpu.roll(hb, nrow - 1, 0))
        return _dot(hid.astype(BF16), w2_ref[...])

    def split_groups(blk):
        g0 = jnp.where(low, blk, 0.0)
        g1 = jnp.where(low, pltpu.roll(blk, HEAD_DIM, 1), 0.0)
        return g0.astype(BF16), g1.astype(BF16)

    kc = _seg_rms(mlp(kc_ref, posk_ref, wk1_ref, wk2_ref), low, gk_ref[0:1, :])
    kco_ref[0, 0], kco_ref[0, 1] = split_groups(kc)
    vc_t = mlp(vc_ref, posv_ref, wv1_ref, wv2_ref).T.astype(BF16)
    vco_ref[0, 0] = vc_t[0:HEAD_DIM]
    vco_ref[0, 1] = vc_t[HEAD_DIM:]


def _compress(kc3, vc3, posk, posv, wk1, wv1, wk2, wv2, gk128):
    batch, nrow, width = kc3.shape
    full = lambda a: pl.BlockSpec(a.shape, lambda b: (0,) * a.ndim)
    in_spec = pl.BlockSpec((1, nrow, width), lambda b: (b, 0, 0))
    k_spec = pl.BlockSpec((1, NSA_GROUPS, nrow, LANES), lambda b: (b, 0, 0, 0))
    vt_spec = pl.BlockSpec((1, NSA_GROUPS, HEAD_DIM, nrow), lambda b: (b, 0, 0, 0))
    return pl.pallas_call(
        functools.partial(_compress_body, nrow=nrow),
        grid=(batch,),
        in_specs=[in_spec, in_spec, full(posk), full(posv), full(wk1), full(wv1),
                  full(wk2), full(wv2), full(gk128)],
        out_specs=(k_spec, vt_spec),
        out_shape=(jax.ShapeDtypeStruct((batch, NSA_GROUPS, nrow, LANES), BF16),
                   jax.ShapeDtypeStruct((batch, NSA_GROUPS, HEAD_DIM, nrow), BF16)),
        compiler_params=_params(("parallel",)),
        name="compress",
    )(kc3, vc3, posk, posv, wk1, wv1, wk2, wv2, gk128)


SEL_CHUNK = 512


def _vt_cols(vt_ref, start, n):
    b0 = start // VT_BLOCK
    return jnp.concatenate([vt_ref[0, b0 + i] for i in range(n // VT_BLOCK)], axis=1)


def _attn_body(qt_ref, kc_ref, vct_ref, ks_ref, vst_ref, kw_ref, vwt_ref, gt_ref, o_ref,
               *, tq, seq, ncmp):
    g = pl.program_id(1)
    t0 = pl.program_id(2) * tq
    cols = NSA_REP * tq
    qt = jnp.concatenate([qt_ref[r] for r in range(NSA_REP)], axis=1)
    q_pad = jnp.concatenate([qt, jnp.zeros((HEAD_DIM, cols), BF16)], axis=0)
    tcol = t0 + (lax.broadcasted_iota(I32, (1, cols), 1) & (tq - 1))
    tlane = t0 + lax.broadcasted_iota(I32, (1, tq), 1)
    per_head = lambda a: jnp.concatenate([a] * NSA_REP, axis=1)

    s = _dot(kc_ref[0, 0], q_pad)
    cpos = lax.broadcasted_iota(I32, (ncmp, 1), 0) * CMP_STRIDE + (CMP_LEN - 1)
    cmask = cpos <= tcol
    s = jnp.where(cmask, s, NEG_INF)
    e = jnp.where(cmask, jnp.exp2(s - jnp.max(s, axis=0, keepdims=True)), 0.0)
    den = jnp.sum(e, axis=0, keepdims=True)
    p_c = e / jnp.where(den > 0.0, den, 1.0)
    o_c = _dot(vct_ref[0, 0], p_c.astype(BF16))

    p_sum = p_c[:, 0:tq]
    for r in range(1, NSA_REP):
        p_sum = p_sum + p_c[:, r * tq:(r + 1) * tq]
    nblk = seq // SEL_BLOCK
    jj = lax.broadcasted_iota(I32, (nblk, ncmp), 0) * SEL_BLOCK
    cc = lax.broadcasted_iota(I32, (nblk, ncmp), 1) * CMP_STRIDE
    ov = jnp.maximum(jnp.minimum(cc + CMP_LEN, jj + SEL_BLOCK) - jnp.maximum(cc, jj), 0)
    ov = (ov.astype(F32) * (1.0 / CMP_LEN)).astype(BF16)
    p1, p2, p3 = _split3(p_sum)
    imp = _dot(ov, p1) + _dot(ov, p2) + _dot(ov, p3)

    jb = lax.broadcasted_iota(I32, (nblk, tq), 0)
    cur = lax.shift_right_logical(t0 + lax.broadcasted_iota(I32, (nblk, tq), 1), 6)
    valid = jb <= cur
    forced = (jb == cur) | (jb == 0)
    imp = jnp.where(forced, jnp.inf, jnp.where(valid, imp, -jnp.inf))
    rank = jnp.zeros((nblk, tq), F32)
    for j2 in range(nblk):
        other = imp[j2:j2 + 1, :]
        ahead = (other > imp) | ((other == imp) & (jb > j2))
        rank = rank + jnp.where(ahead, 1.0, 0.0)
    sel = (rank < float(min(SEL_TOPN, nblk))) & valid
    bias = jnp.where(sel, 0.0, NEG_INF).astype(BF16)
    if nblk < HEAD_DIM:
        bias = jnp.concatenate([bias, jnp.zeros((HEAD_DIM - nblk, tq), BF16)], axis=0)
    q_sel = jnp.concatenate([qt, per_head(bias)], axis=0)

    def sel_step(start, extra, carry):
        m_old, l_old, acc = carry
        sc = _dot(ks_ref[0, pl.ds(start, SEL_CHUNK), :], q_sel)
        if extra is not None:
            sc = sc + extra
        m_new = jnp.maximum(m_old, jnp.max(sc, axis=0, keepdims=True))
        alpha = jnp.exp2(m_old - m_new)
        p = jnp.exp2(sc - m_new)
        l_new = alpha * l_old + jnp.sum(p, axis=0, keepdims=True)
        acc = alpha * acc + _dot(_vt_cols(vst_ref, start, SEL_CHUNK), p.astype(BF16))
        return m_new, l_new, acc

    n_full = t0 // SEL_CHUNK
    init = (jnp.full((1, cols), -3.0e38, F32), jnp.zeros((1, cols), F32), jnp.zeros((HEAD_DIM, cols), F32))
    carry = lax.fori_loop(
        0, n_full, lambda ci, c: sel_step(pl.multiple_of(ci * SEL_CHUNK, SEL_CHUNK), None, c), init)
    tail0 = pl.multiple_of(n_full * SEL_CHUNK, SEL_CHUNK)
    kpos = tail0 + lax.broadcasted_iota(I32, (SEL_CHUNK, 1), 0)
    causal = jnp.where(kpos <= tlane, 0.0, NEG_INF)
    _, l_s, acc_s = sel_step(tail0, per_head(causal), carry)
    o_s = acc_s / l_s

    band = WINDOW + tq
    w0 = pl.multiple_of(jnp.maximum(t0 - WINDOW, 0), tq)
    sw = _dot(kw_ref[0, pl.ds(w0, band), :], q_pad)
    dist = tlane - (w0 + lax.broadcasted_iota(I32, (band, 1), 0))
    in_win = lax.bitcast_convert_type(dist, jnp.uint32) < jnp.uint32(WINDOW)
    sw = sw + per_head(jnp.where(in_win, 0.0, NEG_INF))
    pw = jnp.exp2(sw - jnp.max(sw, axis=0, keepdims=True))
    o_w = _dot(_vt_cols(vwt_ref, w0, band), pw.astype(BF16)) / jnp.sum(pw, axis=0, keepdims=True)

    def gate(r, br):
        col = (g * NSA_REP + r) * NSA_BRANCHES + br
        return jax.nn.sigmoid(gt_ref[pl.ds(col, 1), :])

    heads = []
    for r in range(NSA_REP):
        sl = slice(r * tq, (r + 1) * tq)
        heads.append(gate(r, 0) * o_c[:, sl] + gate(r, 1) * o_s[:, sl] + gate(r, 2) * o_w[:, sl])
    o_ref[...] = jnp.concatenate(heads, axis=0).T.astype(BF16)


def _attention(qt, kcmp, vcmpt, ksel, vselt, kwin, vwint, gates_t, *, batch, seq, tq):
    n_tok = batch * seq
    ncmp = kcmp.shape[2]
    nq = seq // tq
    nvb = seq // VT_BLOCK
    k_spec = pl.BlockSpec((1, seq, LANES), lambda b, g, i: (g, b, 0))
    vt_spec = pl.BlockSpec((1, nvb, HEAD_DIM, VT_BLOCK), lambda b, g, i: (g, b, 0, 0))
    return pl.pallas_call(
        functools.partial(_attn_body, tq=tq, seq=seq, ncmp=ncmp),
        grid=(batch, NSA_GROUPS, nq),
        in_specs=[
            pl.BlockSpec((NSA_REP, HEAD_DIM, tq), lambda b, g, i: (g, 0, b * nq + i)),
            pl.BlockSpec((1, 1, ncmp, LANES), lambda b, g, i: (b, g, 0, 0)),
            pl.BlockSpec((1, 1, HEAD_DIM, ncmp), lambda b, g, i: (b, g, 0, 0)),
            k_spec, vt_spec, k_spec, vt_spec,
            pl.BlockSpec((GATE_ROWS, tq), lambda b, g, i: (0, b * nq + i)),
        ],
        out_specs=pl.BlockSpec((tq, NSA_REP * HEAD_DIM), lambda b, g, i: (b * nq + i, g)),
        out_shape=jax.ShapeDtypeStruct((n_tok, NSA_WIDTH), BF16),
        compiler_params=_params(("parallel", "parallel", "arbitrary")),
        name="nsa_attention",
    )(qt, kcmp, vcmpt, ksel, vselt, kwin, vwint, gates_t)


def _hgrn_body(q_ref, f_ref, i_ref, g_ref, lb_ref, gn_ref, o_ref, st_ref, *, tc):
    C, SUB = HGRN_CHUNK, HGRN_SUB

    @pl.when(pl.program_id(2) == 0)
    def _():
        st_ref[...] = jnp.zeros_like(st_ref)

    r = lax.broadcasted_iota(I32, (C, C), 0)
    c = lax.broadcasted_iota(I32, (C, C), 1)
    assert C // SUB == 8, "the cross-sub-chunk factorisation below is written for 8 sub-chunks"
    sub_shift = SUB.bit_length() - 1
    rs, cs = r >> sub_shift, c >> sub_shift
    one = lambda m: jnp.where(m, 1.0, 0.0).astype(BF16)
    sum_mats = jnp.concatenate([one(c <= r), one((c <= r) & (rs == cs)),
                                one((c > r) & (rs == cs)), one(c > r)], axis=0)
    row = lax.broadcasted_iota(I32, (C, 1), 0)
    rsub = row >> sub_shift
    lb = lb_ref[...]
    gn = gn_ref[...]

    def chunk(ci, carry):
        sl = pl.ds(pl.multiple_of(ci * C, C), C)
        q = _silu(q_ref[sl, :].astype(F32))
        fg = lb + (1.0 - lb) * jax.nn.sigmoid(f_ref[sl, :].astype(F32))
        k = 1.0 - fg
        v = i_ref[sl, :].astype(F32)
        l1, l2, l3 = _split3(jnp.log(fg))
        sums = _dot(sum_mats, l1) + _dot(sum_mats, l2) + _dot(sum_mats, l3)
        a_full = sums[0:C]
        a_sub = sums[C:2 * C]
        b_sub = sums[2 * C:3 * C]
        b_full = sums[3 * C:4 * C]

        out = jnp.sum(q * k, axis=-1, keepdims=True) * v
        for d in range(1, SUB):
            ok = (row & (SUB - 1)) >= d
            arg = jnp.where(ok, a_sub - pltpu.roll(a_sub, d, 0), NEG_INF)
            w = jnp.sum(q * pltpu.roll(k, d, 0) * jnp.exp(arg), axis=-1, keepdims=True)
            out = out + w * pltpu.roll(v, d, 0)

        qd = q * jnp.exp(a_sub)
        kd = k * jnp.exp(b_sub)
        tot = [a_sub[(m + 1) * SUB - 1:(m + 1) * SUB, :] for m in range(C // SUB)]

        def side(terms):
            f = jnp.ones((C, HGRN_DK), F32)
            for sub, ms in terms.items():
                f = jnp.where(rsub == sub, jnp.exp(sum(tot[m] for m in ms)), f)
            return f

        a_adj = _dot_nt(qd.astype(BF16), kd.astype(BF16))
        q_half = qd * side({3: [2], 7: [6]})
        k_half = kd * side({0: [1], 4: [5]})
        a_half = _dot_nt(q_half.astype(BF16), k_half.astype(BF16))
        q_mid = qd * side({5: [4], 6: [4, 5], 7: [4, 5, 6]})
        k_mid = kd * side({2: [3], 1: [2, 3], 0: [1, 2, 3]})
        a_mid = _dot_nt(q_mid.astype(BF16), k_mid.astype(BF16))
        far = rs - cs >= 2
        a_off = jnp.where(rs == cs + 1, a_adj,
                          jnp.where(far & ((rs >> 2) == (cs >> 2)), a_half, jnp.where(far, a_mid, 0.0)))
        vb = v.astype(BF16)
        out = out + _dot(a_off.astype(BF16), vb)

        st = st_ref[...]
        out = out + _dot_nt((q * jnp.exp(a_full)).astype(BF16), st.astype(BF16))
        k_dec = (k * jnp.exp(b_full)).astype(BF16)
        st_ref[...] = st * jnp.exp(a_full[C - 1:C, :]) + _dot_tn(vb, k_dec)

        ms = jnp.mean(out * out, axis=-1, keepdims=True)
        on = out * lax.rsqrt(ms + RMS_EPS) * gn
        o_ref[sl, :] = (on * _silu(g_ref[sl, :].astype(F32))).astype(BF16)
        return carry

    lax.fori_loop(0, tc // C, chunk, 0, unroll=4)


def _hgrn(hg, lb, gn, *, batch, seq, tc):
    n_tok = batch * seq
    nt = seq // tc
    part = lambda p: pl.BlockSpec((tc, HGRN_DK), lambda b, h, i, p=p: (b * nt + i, p * HGRN_HEADS + h))
    return pl.pallas_call(
        functools.partial(_hgrn_body, tc=tc),
        grid=(batch, HGRN_HEADS, nt),
        in_specs=[part(0), part(1), part(2), part(3),
                  pl.BlockSpec((1, HGRN_DK), lambda b, h, i: (0, h)),
                  pl.BlockSpec((1, HGRN_DK), lambda b, h, i: (0, 0))],
        out_specs=pl.BlockSpec((tc, HGRN_DK), lambda b, h, i: (b * nt + i, h)),
        out_shape=jax.ShapeDtypeStruct((n_tok, HGRN_WIDTH), BF16),
        scratch_shapes=[pltpu.VMEM((HGRN_DK, HGRN_DK), F32)],
        compiler_params=_params(("parallel", "parallel", "arbitrary")),
        name="hgrn2",
    )(hg, hg, hg, hg, lb, gn)


def _outproj_body(on_ref, oh_ref, x_ref, gno_ref, wo1_ref, wo2_ref, gf_ref, wr_ref, br_ref,
                  h_ref, hn_ref, route_ref):
    a = on_ref[...].astype(F32)
    ms = jnp.mean(a * a, axis=-1, keepdims=True)
    an = (a * lax.rsqrt(ms + RMS_EPS) * gno_ref[...]).astype(BF16)
    h = x_ref[...] + _dot(an, wo1_ref[...]) + _dot(oh_ref[...], wo2_ref[...])
    h_ref[...] = h
    ms2 = jnp.mean(h * h, axis=-1, keepdims=True)
    hn = h * lax.rsqrt(ms2 + RMS_EPS) * gf_ref[...]
    tm = hn.shape[0]
    for s in range(D_MODEL // LANES):
        hn_ref[pl.ds(s, tm, stride=D_MODEL // LANES), :] = hn[:, s * LANES:(s + 1) * LANES]

    hn_hi = hn.astype(BF16)
    hn_lo = (hn - hn_hi.astype(F32)).astype(BF16)
    lg = (_dot(hn_hi, wr_ref[0]) + _dot(hn_lo, wr_ref[0]) + _dot(hn_hi, wr_ref[1])) + br_ref[...]

    lane = lax.broadcasted_iota(I32, (1, LANES), 1).astype(F32)
    first = lambda hit: jnp.min(jnp.where(hit, lane, 1e9), axis=-1, keepdims=True)
    gmask = lane < MOE_GROUPS
    lgm = jnp.where(gmask, lg, -jnp.inf)
    mg = jnp.max(lgm, axis=-1, keepdims=True)
    pg_top = 1.0 / jnp.sum(jnp.where(gmask, jnp.exp(lgm - mg), 0.0), axis=-1, keepdims=True)
    gidx = first(lgm == mg)
    eloc = lane - MOE_GROUPS
    emask = (eloc >= 0) & (eloc < N_EXPERTS) & (jnp.floor(eloc * (1.0 / EXPERTS_PER_GROUP)) == gidx)
    le1 = jnp.where(emask, lg, -jnp.inf)
    m1 = jnp.max(le1, axis=-1, keepdims=True)
    i1 = first(le1 == m1)
    le2 = jnp.where(lane == i1, -jnp.inf, le1)
    m2 = jnp.max(le2, axis=-1, keepdims=True)
    i2 = first(le2 == m2)
    e2 = jnp.exp(m2 - m1)
    w1 = pg_top / (1.0 + e2)
    w2 = pg_top * e2 / (1.0 + e2)
    route_ref[...] = jnp.where(lane == 0, i1 - MOE_GROUPS,
                     jnp.where(lane == 1, i2 - MOE_GROUPS,
                     jnp.where(lane == 2, w1, jnp.where(lane == 3, w2, 0.0))))


def _outproj(o_nsa, o_hgrn, x2, gno, wo1, wo2, gf, wr, br, *, tm):
    n_tok = x2.shape[0]
    row_spec = lambda cols: pl.BlockSpec((tm, cols), lambda i: (i, 0))
    full = lambda a: pl.BlockSpec(a.shape, lambda i: (0,) * a.ndim)
    return pl.pallas_call(
        _outproj_body,
        grid=(n_tok // tm,),
        in_specs=[row_spec(NSA_WIDTH), row_spec(HGRN_WIDTH), row_spec(D_MODEL), full(gno),
                  full(wo1), full(wo2), full(gf), full(wr), full(br)],
        out_specs=(row_spec(D_MODEL),
                   pl.BlockSpec((tm * (D_MODEL // LANES), LANES), lambda i: (i, 0)),
                   row_spec(LANES)),
        out_shape=(jax.ShapeDtypeStruct((n_tok, D_MODEL), F32),
                   jax.ShapeDtypeStruct((n_tok * (D_MODEL // LANES), LANES), F32),
                   jax.ShapeDtypeStruct((n_tok, LANES), F32)),
        compiler_params=_params(("parallel",)),
        name="outproj_router",
    )(o_nsa, o_hgrn, x2, gno, wo1, wo2, gf, wr, br)


SLAB = D_MODEL // LANES


def _expert_body(be_ref, nb_ref, run_ref, live_ref, order_ref, hn_ref, w1_ref, w3_ref, w2_ref, y_ref,
                 xbuf, ybuf, gsem, ssem, *, n_asg):
    i = pl.program_id(0)
    n_used = nb_ref[0]
    R = MOE_BLOCK
    n_tok = n_asg // 2
    last_blk = pl.num_programs(0) - 1

    def slab(r):
        return pl.ds(r * SLAB if isinstance(r, int) else pl.multiple_of(r * SLAB, SLAB), SLAB)

    def assignment(blk, r):
        a = order_ref[jnp.minimum(run_ref[blk] + r, n_asg - 1)]
        return a, r < live_ref[blk]

    def gather_row(blk, slot, r):
        a, live = assignment(blk, r)
        tok = jnp.where(live, lax.shift_right_logical(a, 1), 0)
        pltpu.make_async_copy(hn_ref.at[tok], xbuf.at[slot, slab(r), :], gsem.at[slot]).start()

    def gather_wait(slot):
        pltpu.make_async_copy(xbuf.at[slot], xbuf.at[slot], gsem.at[slot]).wait()

    def scatter_slab(r, row):
        pltpu.make_async_copy(ybuf.at[slab(r), :], y_ref.at[row], ssem).start()

    def scatter_row(blk, r):
        a, live = assignment(blk, r)
        scatter_slab(r, jnp.where(live, (a & 1) * n_tok + lax.shift_right_logical(a, 1), n_asg + r))

    def scatter_wait():
        pltpu.make_async_copy(ybuf, ybuf, ssem).wait()

    def rolled(fn):
        def body(r, c):
            fn(r)
            return c
        lax.fori_loop(0, R, body, 0, unroll=8)

    @pl.when(i == 0)
    def _():
        ybuf[...] = jnp.zeros_like(ybuf)
        rolled(lambda r: scatter_slab(r, n_asg + r))
        scatter_wait()
        rolled(lambda r: gather_row(0, 0, r))

    @pl.when(i < n_used)
    def _():
        slot = i & 1
        gather_wait(slot)
        x = jnp.concatenate([xbuf[slot, pl.ds(s, R, stride=SLAB), :] for s in range(SLAB)], axis=1).astype(BF16)
        nxt = jnp.minimum(i + 1, n_used - 1)
        prev = jnp.where(i > 0, i - 1, last_blk)
        for r in range(R):
            gather_row(nxt, 1 - slot, r)
            scatter_row(prev, r)
        hid = _silu(_dot(x, w1_ref[0])) * _dot(x, w3_ref[0])
        y = _dot(hid.astype(BF16), w2_ref[0])
        scatter_wait()
        for s in range(SLAB):
            ybuf[pl.ds(s, R, stride=SLAB), :] = y[:, s * LANES:(s + 1) * LANES]

        @pl.when(i == n_used - 1)
        def _():
            rolled(lambda r: scatter_row(i, r))
            scatter_wait()
            gather_wait(1 - slot)


def _experts(blk_e, n_used, run0, n_live, order, hn_slabs, w1, w3, w2):
    n_asg = order.shape[0]
    nblk = blk_e.shape[0]
    wspec = lambda a: pl.BlockSpec((1,) + a.shape[1:], lambda i, be, *_: (be[i], 0, 0))
    grid_spec = pltpu.PrefetchScalarGridSpec(
        num_scalar_prefetch=5,
        grid=(nblk,),
        in_specs=[pl.BlockSpec(memory_space=pl.ANY), wspec(w1), wspec(w3), wspec(w2)],
        out_specs=pl.BlockSpec(memory_space=pl.ANY),
        scratch_shapes=[pltpu.VMEM((2, MOE_BLOCK * SLAB, LANES), F32), pltpu.VMEM((MOE_BLOCK * SLAB, LANES), F32),
                        pltpu.SemaphoreType.DMA((2,)), pltpu.SemaphoreType.DMA],
    )
    return pl.pallas_call(
        functools.partial(_expert_body, n_asg=n_asg),
        grid_spec=grid_spec,
        out_shape=jax.ShapeDtypeStruct((n_asg + MOE_BLOCK, SLAB, LANES), F32),
        compiler_params=_params(("arbitrary",)),
        name="experts",
    )(blk_e, n_used, run0, n_live, order, hn_slabs, w1, w3, w2)


def _combine_body(h_ref, y0_ref, y1_ref, rt_ref, o_ref, *, tm):
    rt = rt_ref[...]
    w0, w1 = rt[:, 2:3], rt[:, 3:4]
    for s in range(SLAB):
        cols = slice(s * LANES, (s + 1) * LANES)
        y0 = y0_ref[pl.ds(s, tm, stride=SLAB), :]
        y1 = y1_ref[pl.ds(s, tm, stride=SLAB), :]
        o_ref[:, cols] = h_ref[:, cols] + (w0 * y0 + w1 * y1)


def _combine(h, y_slabs, route, *, tm):
    n_tok = h.shape[0]
    nb = n_tok // tm
    y2d = y_slabs.reshape(-1, LANES)
    return pl.pallas_call(
        functools.partial(_combine_body, tm=tm),
        grid=(nb,),
        in_specs=[pl.BlockSpec((tm, D_MODEL), lambda i: (i, 0)),
                  pl.BlockSpec((tm * SLAB, LANES), lambda i: (i, 0)),
                  pl.BlockSpec((tm * SLAB, LANES), lambda i: (nb + i, 0)),
                  pl.BlockSpec((tm, LANES), lambda i: (i, 0))],
        out_specs=pl.BlockSpec((tm, D_MODEL), lambda i: (i, 0)),
        out_shape=jax.ShapeDtypeStruct((n_tok, D_MODEL), F32),
        compiler_params=_params(("parallel",)),
        name="moe_combine",
    )(h, y2d, y2d, route)


def _pair_groups(w):
    rows, cols = w.shape
    z = jnp.zeros((NSA_GROUPS, rows, NSA_GROUPS, cols), w.dtype)
    for g in range(NSA_GROUPS):
        z = z.at[g, :, g, :].set(w)
    return z


def _compress_weights(w1, w2, pos):
    w1r = w1.reshape(2, CMP_STRIDE, HEAD_DIM, CMP_HIDDEN)
    halves = []
    for half in range(2):
        z = jnp.zeros((CMP_STRIDE, NSA_GROUPS, HEAD_DIM, NSA_GROUPS, CMP_HIDDEN), F32)
        for g in range(NSA_GROUPS):
            z = z.at[:, g, :, g, :].set(w1r[half])
        halves.append(z.reshape(CMP_STRIDE * KV_DIM, NSA_GROUPS * CMP_HIDDEN))
    w1p = jnp.stack(halves).astype(BF16)
    w2p = _pair_groups(w2).reshape(NSA_GROUPS * CMP_HIDDEN, KV_DIM).astype(BF16)
    posr = pos.reshape(2, CMP_STRIDE, 1, HEAD_DIM)
    posp = jnp.broadcast_to(posr, (2, CMP_STRIDE, NSA_GROUPS, HEAD_DIM)).reshape(2, CMP_STRIDE * KV_DIM)
    return w1p, w2p, posp


def kernel(x, attn_norm_g, ffn_norm_g, w_in, w_out, nsa_q_norm_g, nsa_k_norm_g, cmp_pos_k, cmp_pos_v,
           cmp_wk1, cmp_wk2, cmp_wv1, cmp_wv2, nsa_out_norm_g, hgrn_lb_logits, hgrn_out_norm_g,
           moe_w_group, moe_b_group, moe_w_expert, moe_b_expert, moe_w1, moe_w3, moe_w2):
    batch, seq, _ = x.shape
    n_tok = batch * seq
    assert seq % SEL_CHUNK == 0 and seq >= WINDOW + 128 and seq // SEL_BLOCK <= HEAD_DIM
    depth = w_in.shape[0]
    lb_all = jnp.cumsum(jax.nn.softmax(hgrn_lb_logits.astype(F32), axis=0), axis=0)
    tm = min(512, n_tok)
    h = x.reshape(n_tok, D_MODEL)
    for layer in range(depth):
        wl = w_in[layer]
        gate_cols = wl[:, QKV_COLS:QKV_COLS + NSA_HEADS * NSA_BRANCHES]
        w_in_p = jnp.concatenate(
            [wl[:, :QKV_COLS], wl[:, QKV_COLS + NSA_HEADS * NSA_BRANCHES:],
             jnp.pad(gate_cols, ((0, 0), (0, LANES - NSA_HEADS * NSA_BRANCHES)))], axis=1).astype(BF16)
        gq128 = jnp.tile(nsa_q_norm_g[layer], 2)[None, :]
        gk128 = jnp.tile(nsa_k_norm_g[layer], (1, 2))
        g_attn = attn_norm_g[layer][None, :]

        qt, ksel, vselt, kwin, vwint, kc_raw, vc_raw, gates_t, hg = _inproj(
            h, g_attn, w_in_p, gq128, gk128, seq=seq, tm=tm)

        wk1p, wk2p, poskp = _compress_weights(cmp_wk1[layer], cmp_wk2[layer], cmp_pos_k[layer])
        wv1p, wv2p, posvp = _compress_weights(cmp_wv1[layer], cmp_wv2[layer], cmp_pos_v[layer])
        nrow = seq // CMP_STRIDE
        kcmp, vcmpt = _compress(kc_raw.reshape(batch, nrow, CMP_STRIDE * KV_DIM),
                                vc_raw.reshape(batch, nrow, CMP_STRIDE * KV_DIM),
                                poskp, posvp, wk1p, wv1p, wk2p, wv2p, gk128)

        o_nsa = _attention(qt, kcmp, vcmpt, ksel, vselt, kwin, vwint, gates_t,
                           batch=batch, seq=seq, tq=256)
        o_hgrn = _hgrn(hg, lb_all[layer][None, :], hgrn_out_norm_g[layer][None, :],
                       batch=batch, seq=seq, tc=min(512, seq))

        wo = w_out[layer].astype(BF16)
        wr = jnp.concatenate([moe_w_group[layer], moe_w_expert[layer]], axis=1)
        wr = jnp.pad(wr, ((0, 0), (0, LANES - wr.shape[1])))
        wr_hi = wr.astype(BF16)
        wr_lo = (wr - wr_hi.astype(F32)).astype(BF16)
        br = jnp.pad(jnp.concatenate([moe_b_group[layer], moe_b_expert[layer]]),
                     (0, LANES - MOE_GROUPS - N_EXPERTS))[None, :]
        h, hn, route = _outproj(o_nsa, o_hgrn, h, nsa_out_norm_g[layer][None, :],
                                wo[:NSA_WIDTH], wo[NSA_WIDTH:], ffn_norm_g[layer][None, :],
                                jnp.stack([wr_hi, wr_lo]), br, tm=tm)

        e_flat = route[:, 0:2].astype(I32).reshape(-1)
        n_asg = e_flat.shape[0]
        order = jnp.argsort(e_flat, stable=True).astype(I32)
        counts = jnp.sum((e_flat[:, None] == jnp.arange(N_EXPERTS)[None, :]).astype(I32), axis=0)
        start = jnp.cumsum(counts) - counts
        padded = (counts + MOE_BLOCK - 1) // MOE_BLOCK * MOE_BLOCK
        pend = jnp.cumsum(padded)
        p_rows = n_asg + N_EXPERTS * MOE_BLOCK
        nblk = p_rows // MOE_BLOCK
        blk_e = jnp.minimum(jnp.sum(jnp.arange(nblk)[:, None] * MOE_BLOCK >= pend[None, :], axis=1),
                            N_EXPERTS - 1).astype(I32)
        n_used = (pend[-1] // MOE_BLOCK).astype(I32)[None]
        r0 = jnp.arange(nblk, dtype=I32) * MOE_BLOCK - (pend - padded)[blk_e]
        n_live = jnp.clip(counts[blk_e] - r0, 0, MOE_BLOCK).astype(I32)
        run0 = jnp.clip(start[blk_e] + r0, 0, n_asg).astype(I32)
        y_slabs = _experts(blk_e, n_used, run0, n_live, order, hn.reshape(n_tok, SLAB, LANES),
                           moe_w1[layer].astype(BF16), moe_w3[layer].astype(BF16), moe_w2[layer].astype(BF16))
        h = _combine(h, y_slabs, route, tm=min(256, n_tok))
    return h.reshape(batch, seq, D_MODEL)
```

```python
import functools

import jax
import jax.numpy as jnp
import numpy as np
from jax import lax
from jax.experimental import pallas as pl
from jax.experimental.pallas import tpu as pltpu

F32 = jnp.float32
BF16 = jnp.bfloat16
I32 = jnp.int32

D_MODEL = 1024
NSA_HEADS = 8
NSA_GROUPS = 2
NSA_REP = NSA_HEADS // NSA_GROUPS
HEAD_DIM = 64
NSA_WIDTH = NSA_HEADS * HEAD_DIM
KV_DIM = NSA_GROUPS * HEAD_DIM
NSA_BRANCHES = 3
CMP_STRIDE = 16
CMP_LEN = 32
CMP_HIDDEN = 256
SEL_BLOCK = 64
SEL_TOPN = 16
WINDOW = 512
HGRN_WIDTH = D_MODEL - NSA_WIDTH
HGRN_HEADS = 4
HGRN_DK = 128
HGRN_CHUNK = 64
HGRN_SUB = 8
MOE_GROUPS = 4
EXPERTS_PER_GROUP = 8
N_EXPERTS = MOE_GROUPS * EXPERTS_PER_GROUP
EXPERT_FF = 512
MOE_BLOCK = 256
RMS_EPS = 1e-6
NEG_INF = -1e30

LANES = 128
QKV_COLS = NSA_WIDTH + 6 * KV_DIM
HG_COLS = 4 * HGRN_WIDTH
IN_COLS_PAD = QKV_COLS + HG_COLS + LANES
GATE_ROWS = 32
VT_BLOCK = 128
LOG2E = 1.4426950408889634
VMEM_LIMIT = 56 * 1024 * 1024


def _dot(a, b):
    return jnp.dot(a, b, preferred_element_type=F32)


def _dot_nt(a, b):
    return lax.dot_general(a, b, (((1,), (1,)), ((), ())), preferred_element_type=F32)


def _dot_tn(a, b):
    return lax.dot_general(a, b, (((0,), (0,)), ((), ())), preferred_element_type=F32)


def _split3(x):
    a = x.astype(BF16)
    r = x - a.astype(F32)
    b = r.astype(BF16)
    c = (r - b.astype(F32)).astype(BF16)
    return a, b, c


def _silu(x):
    return x * jax.nn.sigmoid(x)


def _params(sem):
    return pltpu.CompilerParams(dimension_semantics=sem, vmem_limit_bytes=VMEM_LIMIT)


def _seg_rms(blk, low, gain):
    sq = blk * blk
    s_lo = jnp.sum(jnp.where(low, sq, 0.0), axis=-1, keepdims=True)
    s_hi = jnp.sum(jnp.where(low, 0.0, sq), axis=-1, keepdims=True)
    inv = jnp.where(low, lax.rsqrt(s_lo * (1.0 / HEAD_DIM) + RMS_EPS),
                    lax.rsqrt(s_hi * (1.0 / HEAD_DIM) + RMS_EPS))
    return blk * inv * gain


def _inproj_body(x_ref, g_ref, w_ref, gq_ref, gk_ref,
                 qt_ref, ksel_ref, vselt_ref, kwin_ref, vwint_ref, kc_ref, vc_ref,
                 gates_ref, hg_ref, *, tm, seq):
    x = x_ref[...]
    ms = jnp.mean(x * x, axis=-1, keepdims=True)
    n = (x * lax.rsqrt(ms + RMS_EPS) * g_ref[...]).astype(BF16)
    y = _dot(n, w_ref[...])

    lane = lax.broadcasted_iota(I32, (1, LANES), 1)
    low = lane < HEAD_DIM
    gq = gq_ref[...]
    scale = HEAD_DIM ** -0.5 * LOG2E
    for j in range(NSA_HEADS // 2):
        nrm_t = (_seg_rms(y[:, LANES * j:LANES * (j + 1)], low, gq) * scale).T.astype(BF16)
        qt_ref[2 * j] = nrm_t[0:HEAD_DIM]
        qt_ref[2 * j + 1] = nrm_t[HEAD_DIM:]

    def kv_block(idx):
        c0 = NSA_WIDTH + LANES * idx
        return y[:, c0:c0 + LANES]

    def split_groups(blk, hi_fill):
        g0 = jnp.where(low, blk, hi_fill)
        g1 = jnp.where(low, pltpu.roll(blk, HEAD_DIM, 1), hi_fill)
        return g0.astype(BF16), g1.astype(BF16)

    def store_transposed(vt_ref, blk):
        blk_t = blk.T.astype(BF16)
        for g in range(NSA_GROUPS):
            for c in range(tm // VT_BLOCK):
                vt_ref[g, c] = blk_t[g * HEAD_DIM:(g + 1) * HEAD_DIM, c * VT_BLOCK:(c + 1) * VT_BLOCK]

    kc_ref[...] = kv_block(0).astype(BF16)
    vc_ref[...] = kv_block(1).astype(BF16)

    row = lax.broadcasted_iota(I32, (tm, LANES), 0)
    t = (pl.program_id(0) * tm) % seq + row
    onehot = jnp.where(lane - HEAD_DIM == lax.shift_right_logical(t, 6), 1.0, 0.0)
    ks = _seg_rms(kv_block(2), low, gk_ref[1:2, :])
    ksel_ref[0], ksel_ref[1] = split_groups(ks, onehot)
    store_transposed(vselt_ref, kv_block(3))
    kw = _seg_rms(kv_block(4), low, gk_ref[2:3, :])
    kwin_ref[0], kwin_ref[1] = split_groups(kw, 0.0)
    store_transposed(vwint_ref, kv_block(5))

    hg_ref[...] = y[:, QKV_COLS:QKV_COLS + HG_COLS].astype(BF16)
    gates_ref[...] = y[:, QKV_COLS + HG_COLS:].T[0:GATE_ROWS]


def _inproj(x2, g_attn, w_in_p, gq128, gk128, *, seq, tm):
    n_tok = x2.shape[0]
    grid = (n_tok // tm,)
    row_spec = lambda cols: pl.BlockSpec((tm, cols), lambda i: (i, 0))
    k_spec = pl.BlockSpec((NSA_GROUPS, tm, LANES), lambda i: (0, i, 0))
    vt_spec = pl.BlockSpec((NSA_GROUPS, tm // VT_BLOCK, HEAD_DIM, VT_BLOCK), lambda i: (0, i, 0, 0))
    vt_shape = jax.ShapeDtypeStruct((NSA_GROUPS, n_tok // VT_BLOCK, HEAD_DIM, VT_BLOCK), BF16)
    full = lambda a: pl.BlockSpec(a.shape, lambda i: (0,) * a.ndim)
    out_shape = (
        jax.ShapeDtypeStruct((NSA_HEADS, HEAD_DIM, n_tok), BF16),
        jax.ShapeDtypeStruct((NSA_GROUPS, n_tok, LANES), BF16),
        vt_shape,
        jax.ShapeDtypeStruct((NSA_GROUPS, n_tok, LANES), BF16),
        vt_shape,
        jax.ShapeDtypeStruct((n_tok, LANES), BF16),
        jax.ShapeDtypeStruct((n_tok, LANES), BF16),
        jax.ShapeDtypeStruct((GATE_ROWS, n_tok), F32),
        jax.ShapeDtypeStruct((n_tok, HG_COLS), BF16),
    )
    out_specs = (pl.BlockSpec((NSA_HEADS, HEAD_DIM, tm), lambda i: (0, 0, i)),
                 k_spec, vt_spec, k_spec, vt_spec, row_spec(LANES), row_spec(LANES),
                 pl.BlockSpec((GATE_ROWS, tm), lambda i: (0, i)), row_spec(HG_COLS))
    return pl.pallas_call(
        functools.partial(_inproj_body, tm=tm, seq=seq),
        grid=grid,
        in_specs=[row_spec(D_MODEL), full(g_attn), full(w_in_p), full(gq128), full(gk128)],
        out_specs=out_specs,
        out_shape=out_shape,
        compiler_params=_params(("parallel",)),
        name="inproj",
    )(x2, g_attn, w_in_p, gq128, gk128)


def _compress_body(kc_ref, vc_ref, posk_ref, posv_ref, wk1_ref, wv1_ref, wk2_ref, wv2_ref,
                   gk_ref, kco_ref, vco_ref, *, nrow):
    lane = lax.broadcasted_iota(I32, (1, LANES), 1)
    low = lane < HEAD_DIM

    def mlp(x_ref, pos_ref, w1_ref, w2_ref):
        x = x_ref[0].astype(F32)
        ha = _dot((x + pos_ref[0:1, :]).astype(BF16), w1_ref[0])
        hb = _dot((x + pos_ref[1:2, :]).astype(BF16), w1_ref[1])
        hid = _silu(ha + pltpu.roll(hb, nrow - 1, 0))
        return _dot(hid.astype(BF16), w2_ref[...])

    def split_groups(blk):
        g0 = jnp.where(low, blk, 0.0)
        g1 = jnp.where(low, pltpu.roll(blk, HEAD_DIM, 1), 0.0)
        return g0.astype(BF16), g1.astype(BF16)

    kc = _seg_rms(mlp(kc_ref, posk_ref, wk1_ref, wk2_ref), low, gk_ref[0:1, :])
    kco_ref[0, 0], kco_ref[0, 1] = split_groups(kc)
    vc_t = mlp(vc_ref, posv_ref, wv1_ref, wv2_ref).T.astype(BF16)
    vco_ref[0, 0] = vc_t[0:HEAD_DIM]
    vco_ref[0, 1] = vc_t[HEAD_DIM:]


def _compress(kc3, vc3, posk, posv, wk1, wv1, wk2, wv2, gk128):
    batch, nrow, width = kc3.shape
    full = lambda a: pl.BlockSpec(a.shape, lambda b: (0,) * a.ndim)
    in_spec = pl.BlockSpec((1, nrow, width), lambda b: (b, 0, 0))
    k_spec = pl.BlockSpec((1, NSA_GROUPS, nrow, LANES), lambda b: (b, 0, 0, 0))
    vt_spec = pl.BlockSpec((1, NSA_GROUPS, HEAD_DIM, nrow), lambda b: (b, 0, 0, 0))
    return pl.pallas_call(
        functools.partial(_compress_body, nrow=nrow),
        grid=(batch,),
        in_specs=[in_spec, in_spec, full(posk), full(posv), full(wk1), full(wv1),
                  full(wk2), full(wv2), full(gk128)],
        out_specs=(k_spec, vt_spec),
        out_shape=(jax.ShapeDtypeStruct((batch, NSA_GROUPS, nrow, LANES), BF16),
                   jax.ShapeDtypeStruct((batch, NSA_GROUPS, HEAD_DIM, nrow), BF16)),
        compiler_params=_params(("parallel",)),
        name="compress",
    )(kc3, vc3, posk, posv, wk1, wv1, wk2, wv2, gk128)


SEL_CHUNK = 512


def _vt_cols(vt_ref, start, n):
    b0 = start // VT_BLOCK
    return jnp.concatenate([vt_ref[0, b0 + i] for i in range(n // VT_BLOCK)], axis=1)


def _attn_body(qt_ref, kc_ref, vct_ref, ks_ref, vst_ref, kw_ref, vwt_ref, gt_ref, o_ref,
               *, tq, seq, ncmp):
    g = pl.program_id(1)
    t0 = pl.program_id(2) * tq
    cols = NSA_REP * tq
    qt = jnp.concatenate([qt_ref[r] for r in range(NSA_REP)], axis=1)
    q_pad = jnp.concatenate([qt, jnp.zeros((HEAD_DIM, cols), BF16)], axis=0)
    tcol = t0 + (lax.broadcasted_iota(I32, (1, cols), 1) & (tq - 1))
    tlane = t0 + lax.broadcasted_iota(I32, (1, tq), 1)
    per_head = lambda a: jnp.concatenate([a] * NSA_REP, axis=1)

    s = _dot(kc_ref[0, 0], q_pad)
    cpos = lax.broadcasted_iota(I32, (ncmp, 1), 0) * CMP_STRIDE + (CMP_LEN - 1)
    cmask = cpos <= tcol
    s = jnp.where(cmask, s, NEG_INF)
    e = jnp.where(cmask, jnp.exp2(s - jnp.max(s, axis=0, keepdims=True)), 0.0)
    den = jnp.sum(e, axis=0, keepdims=True)
    p_c = e / jnp.where(den > 0.0, den, 1.0)
    o_c = _dot(vct_ref[0, 0], p_c.astype(BF16))

    p_sum = p_c[:, 0:tq]
    for r in range(1, NSA_REP):
        p_sum = p_sum + p_c[:, r * tq:(r + 1) * tq]
    nblk = seq // SEL_BLOCK
    jj = lax.broadcasted_iota(I32, (nblk, ncmp), 0) * SEL_BLOCK
    cc = lax.broadcasted_iota(I32, (nblk, ncmp), 1) * CMP_STRIDE
    ov = jnp.maximum(jnp.minimum(cc + CMP_LEN, jj + SEL_BLOCK) - jnp.maximum(cc, jj), 0)
    ov = (ov.astype(F32) * (1.0 / CMP_LEN)).astype(BF16)
    p1, p2, p3 = _split3(p_sum)
    imp = _dot(ov, p1) + _dot(ov, p2) + _dot(ov, p3)

    jb = lax.broadcasted_iota(I32, (nblk, tq), 0)
    cur = lax.shift_right_logical(t0 + lax.broadcasted_iota(I32, (nblk, tq), 1), 6)
    valid = jb <= cur
    forced = (jb == cur) | (jb == 0)
    imp = jnp.where(forced, jnp.inf, jnp.where(valid, imp, -jnp.inf))
    rank = jnp.zeros((nblk, tq), F32)
    for j2 in range(nblk):
        other = imp[j2:j2 + 1, :]
        ahead = (other > imp) | ((other == imp) & (jb > j2))
        rank = rank + jnp.where(ahead, 1.0, 0.0)
    sel = (rank < float(min(SEL_TOPN, nblk))) & valid
    bias = jnp.where(sel, 0.0, NEG_INF).astype(BF16)
    if nblk < HEAD_DIM:
        bias = jnp.concatenate([bias, jnp.zeros((HEAD_DIM - nblk, tq), BF16)], axis=0)
    q_sel = jnp.concatenate([qt, per_head(bias)], axis=0)

    def sel_step(start, extra, carry):
        m_old, l_old, acc = carry
        sc = _dot(ks_ref[0, pl.ds(start, SEL_CHUNK), :], q_sel)
        if extra is not None:
            sc = sc + extra
        m_new = jnp.maximum(m_old, jnp.max(sc, axis=0, keepdims=True))
        alpha = jnp.exp2(m_old - m_new)
        p = jnp.exp2(sc - m_new)
        l_new = alpha * l_old + jnp.sum(p, axis=0, keepdims=True)
        acc = alpha * acc + _dot(_vt_cols(vst_ref, start, SEL_CHUNK), p.astype(BF16))
        return m_new, l_new, acc

    n_full = t0 // SEL_CHUNK
    init = (jnp.full((1, cols), -3.0e38, F32), jnp.zeros((1, cols), F32), jnp.zeros((HEAD_DIM, cols), F32))
    carry = lax.fori_loop(
        0, n_full, lambda ci, c: sel_step(pl.multiple_of(ci * SEL_CHUNK, SEL_CHUNK), None, c), init)
    tail0 = pl.multiple_of(n_full * SEL_CHUNK, SEL_CHUNK)
    kpos = tail0 + lax.broadcasted_iota(I32, (SEL_CHUNK, 1), 0)
    causal = jnp.where(kpos <= tlane, 0.0, NEG_INF)
    _, l_s, acc_s = sel_step(tail0, per_head(causal), carry)
    o_s = acc_s / l_s

    band = WINDOW + tq
    w0 = pl.multiple_of(jnp.maximum(t0 - WINDOW, 0), tq)
    sw = _dot(kw_ref[0, pl.ds(w0, band), :], q_pad)
    dist = tlane - (w0 + lax.broadcasted_iota(I32, (band, 1), 0))
    in_win = lax.bitcast_convert_type(dist, jnp.uint32) < jnp.uint32(WINDOW)
    sw = sw + per_head(jnp.where(in_win, 0.0, NEG_INF))
    pw = jnp.exp2(sw - jnp.max(sw, axis=0, keepdims=True))
    o_w = _dot(_vt_cols(vwt_ref, w0, band), pw.astype(BF16)) / jnp.sum(pw, axis=0, keepdims=True)

    def gate(r, br):
        col = (g * NSA_REP + r) * NSA_BRANCHES + br
        return jax.nn.sigmoid(gt_ref[pl.ds(col, 1), :])

    heads = []
    for r in range(NSA_REP):
        sl = slice(r * tq, (r + 1) * tq)
        heads.append(gate(r, 0) * o_c[:, sl] + gate(r, 1) * o_s[:, sl] + gate(r, 2) * o_w[:, sl])
    o_ref[...] = jnp.concatenate(heads, axis=0).T.astype(BF16)


def _attention(qt, kcmp, vcmpt, ksel, vselt, kwin, vwint, gates_t, *, batch, seq, tq):
    n_tok = batch * seq
    ncmp = kcmp.shape[2]
    nq = seq // tq
    nvb = seq // VT_BLOCK
    k_spec = pl.BlockSpec((1, seq, LANES), lambda b, g, i: (g, b, 0))
    vt_spec = pl.BlockSpec((1, nvb, HEAD_DIM, VT_BLOCK), lambda b, g, i: (g, b, 0, 0))
    return pl.pallas_call(
        functools.partial(_attn_body, tq=tq, seq=seq, ncmp=ncmp),
        grid=(batch, NSA_GROUPS, nq),
        in_specs=[
            pl.BlockSpec((NSA_REP, HEAD_DIM, tq), lambda b, g, i: (g, 0, b * nq + i)),
            pl.BlockSpec((1, 1, ncmp, LANES), lambda b, g, i: (b, g, 0, 0)),
            pl.BlockSpec((1, 1, HEAD_DIM, ncmp), lambda b, g, i: (b, g, 0, 0)),
            k_spec, vt_spec, k_spec, vt_spec,
            pl.BlockSpec((GATE_ROWS, tq), lambda b, g, i: (0, b * nq + i)),
        ],
        out_specs=pl.BlockSpec((tq, NSA_REP * HEAD_DIM), lambda b, g, i: (b * nq + i, g)),
        out_shape=jax.ShapeDtypeStruct((n_tok, NSA_WIDTH), BF16),
        compiler_params=_params(("parallel", "parallel", "arbitrary")),
        name="nsa_attention",
    )(qt, kcmp, vcmpt, ksel, vselt, kwin, vwint, gates_t)


def _hgrn_body(q_ref, f_ref, i_ref, g_ref, lb_ref, gn_ref, o_ref, st_ref, *, tc):
    C, SUB = HGRN_CHUNK, HGRN_SUB

    @pl.when(pl.program_id(2) == 0)
    def _():
        st_ref[...] = jnp.zeros_like(st_ref)

    r = lax.broadcasted_iota(I32, (C, C), 0)
    c = lax.broadcasted_iota(I32, (C, C), 1)
    assert C // SUB == 8, "the cross-sub-chunk factorisation below is written for 8 sub-chunks"
    sub_shift = SUB.bit_length() - 1
    rs, cs = r >> sub_shift, c >> sub_shift
    one = lambda m: jnp.where(m, 1.0, 0.0).astype(BF16)
    sum_mats = jnp.concatenate([one(c <= r), one((c <= r) & (rs == cs)),
                                one((c > r) & (rs == cs)), one(c > r)], axis=0)
    row = lax.broadcasted_iota(I32, (C, 1), 0)
    rsub = row >> sub_shift
    lb = lb_ref[...]
    gn = gn_ref[...]

    def chunk(ci, carry):
        sl = pl.ds(pl.multiple_of(ci * C, C), C)
        q = _silu(q_ref[sl, :].astype(F32))
        fg = lb + (1.0 - lb) * jax.nn.sigmoid(f_ref[sl, :].astype(F32))
        k = 1.0 - fg
        v = i_ref[sl, :].astype(F32)
        l1, l2, l3 = _split3(jnp.log(fg))
        sums = _dot(sum_mats, l1) + _dot(sum_mats, l2) + _dot(sum_mats, l3)
        a_full = sums[0:C]
        a_sub = sums[C:2 * C]
        b_sub = sums[2 * C:3 * C]
        b_full = sums[3 * C:4 * C]

        out = jnp.sum(q * k, axis=-1, keepdims=True) * v
        for d in range(1, SUB):
            ok = (row & (SUB - 1)) >= d
            arg = jnp.where(ok, a_sub - pltpu.roll(a_sub, d, 0), NEG_INF)
            w = jnp.sum(q * pltpu.roll(k, d, 0) * jnp.exp(arg), axis=-1, keepdims=True)
            out = out + w * pltpu.roll(v, d, 0)

        qd = q * jnp.exp(a_sub)
        kd = k * jnp.exp(b_sub)
        tot = [a_sub[(m + 1) * SUB - 1:(m + 1) * SUB, :] for m in range(C // SUB)]

        def side(terms):
            f = jnp.ones((C, HGRN_DK), F32)
            for sub, ms in terms.items():
                f = jnp.where(rsub == sub, jnp.exp(sum(tot[m] for m in ms)), f)
            return f

        a_adj = _dot_nt(qd.astype(BF16), kd.astype(BF16))
        q_half = qd * side({3: [2], 7: [6]})
        k_half = kd * side({0: [1], 4: [5]})
        a_half = _dot_nt(q_half.astype(BF16), k_half.astype(BF16))
        q_mid = qd * side({5: [4], 6: [4, 5], 7: [4, 5, 6]})
        k_mid = kd * side({2: [3], 1: [2, 3], 0: [1, 2, 3]})
        a_mid = _dot_nt(q_mid.astype(BF16), k_mid.astype(BF16))
        far = rs - cs >= 2
        a_off = jnp.where(rs == cs + 1, a_adj,
                          jnp.where(far & ((rs >> 2) == (cs >> 2)), a_half, jnp.where(far, a_mid, 0.0)))
        vb = v.astype(BF16)
        out = out + _dot(a_off.astype(BF16), vb)

        st = st_ref[...]
        out = out + _dot_nt((q * jnp.exp(a_full)).astype(BF16), st.astype(BF16))
        k_dec = (k * jnp.exp(b_full)).astype(BF16)
        st_ref[...] = st * jnp.exp(a_full[C - 1:C, :]) + _dot_tn(vb, k_dec)

        ms = jnp.mean(out * out, axis=-1, keepdims=True)
        on = out * lax.rsqrt(ms + RMS_EPS) * gn
        o_ref[sl, :] = (on * _silu(g_ref[sl, :].astype(F32))).astype(BF16)
        return carry

    lax.fori_loop(0, tc // C, chunk, 0, unroll=4)


def _hgrn(hg, lb, gn, *, batch, seq, tc):
    n_tok = batch * seq
    nt = seq // tc
    part = lambda p: pl.BlockSpec((tc, HGRN_DK), lambda b, h, i, p=p: (b * nt + i, p * HGRN_HEADS + h))
    return pl.pallas_call(
        functools.partial(_hgrn_body, tc=tc),
        grid=(batch, HGRN_HEADS, nt),
        in_specs=[part(0), part(1), part(2), part(3),
                  pl.BlockSpec((1, HGRN_DK), lambda b, h, i: (0, h)),
                  pl.BlockSpec((1, HGRN_DK), lambda b, h, i: (0, 0))],
        out_specs=pl.BlockSpec((tc, HGRN_DK), lambda b, h, i: (b * nt + i, h)),
        out_shape=jax.ShapeDtypeStruct((n_tok, HGRN_WIDTH), BF16),
        scratch_shapes=[pltpu.VMEM((HGRN_DK, HGRN_DK), F32)],
        compiler_params=_params(("parallel", "parallel", "arbitrary")),
        name="hgrn2",
    )(hg, hg, hg, hg, lb, gn)


def _outproj_body(on_ref, oh_ref, x_ref, gno_ref, wo1_ref, wo2_ref, gf_ref, wr_ref, br_ref,
                  h_ref, hn_ref, route_ref):
    a = on_ref[...].astype(F32)
    ms = jnp.mean(a * a, axis=-1, keepdims=True)
    an = (a * lax.rsqrt(ms + RMS_EPS) * gno_ref[...]).astype(BF16)
    h = x_ref[...] + _dot(an, wo1_ref[...]) + _dot(oh_ref[...], wo2_ref[...])
    h_ref[...] = h
    ms2 = jnp.mean(h * h, axis=-1, keepdims=True)
    hn = h * lax.rsqrt(ms2 + RMS_EPS) * gf_ref[...]
    tm = hn.shape[0]
    for s in range(D_MODEL // LANES):
        hn_ref[pl.ds(s, tm, stride=D_MODEL // LANES), :] = hn[:, s * LANES:(s + 1) * LANES]

    hn_hi = hn.astype(BF16)
    hn_lo = (hn - hn_hi.astype(F32)).astype(BF16)
    lg = (_dot(hn_hi, wr_ref[0]) + _dot(hn_lo, wr_ref[0]) + _dot(hn_hi, wr_ref[1])) + br_ref[...]

    lane = lax.broadcasted_iota(I32, (1, LANES), 1).astype(F32)
    first = lambda hit: jnp.min(jnp.where(hit, lane, 1e9), axis=-1, keepdims=True)
    gmask = lane < MOE_GROUPS
    lgm = jnp.where(gmask, lg, -jnp.inf)
    mg = jnp.max(lgm, axis=-1, keepdims=True)
    pg_top = 1.0 / jnp.sum(jnp.where(gmask, jnp.exp(lgm - mg), 0.0), axis=-1, keepdims=True)
    gidx = first(lgm == mg)
    eloc = lane - MOE_GROUPS
    emask = (eloc >= 0) & (eloc < N_EXPERTS) & (jnp.floor(eloc * (1.0 / EXPERTS_PER_GROUP)) == gidx)
    le1 = jnp.where(emask, lg, -jnp.inf)
    m1 = jnp.max(le1, axis=-1, keepdims=True)
    i1 = first(le1 == m1)
    le2 = jnp.where(lane == i1, -jnp.inf, le1)
    m2 = jnp.max(le2, axis=-1, keepdims=True)
    i2 = first(le2 == m2)
    e2 = jnp.exp(m2 - m1)
    w1 = pg_top / (1.0 + e2)
    w2 = pg_top * e2 / (1.0 + e2)
    route_ref[...] = jnp.where(lane == 0, i1 - MOE_GROUPS,
                     jnp.where(lane == 1, i2 - MOE_GROUPS,
                     jnp.where(lane == 2, w1, jnp.where(lane == 3, w2, 0.0))))


def _outproj(o_nsa, o_hgrn, x2, gno, wo1, wo2, gf, wr, br, *, tm):
    n_tok = x2.shape[0]
    row_spec = lambda cols: pl.BlockSpec((tm, cols), lambda i: (i, 0))
    full = lambda a: pl.BlockSpec(a.shape, lambda i: (0,) * a.ndim)
    return pl.pallas_call(
        _outproj_body,
        grid=(n_tok // tm,),
        in_specs=[row_spec(NSA_WIDTH), row_spec(HGRN_WIDTH), row_spec(D_MODEL), full(gno),
                  full(wo1), full(wo2), full(gf), full(wr), full(br)],
        out_specs=(row_spec(D_MODEL),
                   pl.BlockSpec((tm * (D_MODEL // LANES), LANES), lambda i: (i, 0)),
                   row_spec(LANES)),
        out_shape=(jax.ShapeDtypeStruct((n_tok, D_MODEL), F32),
                   jax.ShapeDtypeStruct((n_tok * (D_MODEL // LANES), LANES), F32),
                   jax.ShapeDtypeStruct((n_tok, LANES), F32)),
        compiler_params=_params(("parallel",)),
        name="outproj_router",
    )(o_nsa, o_hgrn, x2, gno, wo1, wo2, gf, wr, br)


SLAB = D_MODEL // LANES


def _expert_body(be_ref, nb_ref, run_ref, live_ref, order_ref, hn_ref, w1_ref, w3_ref, w2_ref, y_ref,
                 xbuf, ybuf, gsem, ssem, *, n_asg):
    i = pl.program_id(0)
    n_used = nb_ref[0]
    R = MOE_BLOCK
    n_tok = n_asg // 2
    last_blk = pl.num_programs(0) - 1

    def slab(r):
        return pl.ds(r * SLAB if isinstance(r, int) else pl.multiple_of(r * SLAB, SLAB), SLAB)

    def dma_priority(r):
        return r % 2 if isinstance(r, int) else 0

    def assignment(blk, r):
        a = order_ref[jnp.minimum(run_ref[blk] + r, n_asg - 1)]
        return a, r < live_ref[blk]

    def gather_row(blk, slot, r):
        a, live = assignment(blk, r)
        tok = jnp.where(live, lax.shift_right_logical(a, 1), 0)
        pltpu.make_async_copy(hn_ref.at[tok], xbuf.at[slot, slab(r), :], gsem.at[slot]).start(
            priority=dma_priority(r))

    def gather_wait(slot):
        pltpu.make_async_copy(xbuf.at[slot], xbuf.at[slot], gsem.at[slot]).wait()

    def scatter_slab(r, row):
        pltpu.make_async_copy(ybuf.at[slab(r), :], y_ref.at[row], ssem).start(priority=dma_priority(r))

    def scatter_row(blk, r):
        a, live = assignment(blk, r)
        scatter_slab(r, jnp.where(live, (a & 1) * n_tok + lax.shift_right_logical(a, 1), n_asg + r))

    def scatter_wait():
        pltpu.make_async_copy(ybuf, ybuf, ssem).wait()

    def rolled(fn):
        def body(r, c):
            fn(r)
            return c
        lax.fori_loop(0, R, body, 0, unroll=8)

    @pl.when(i == 0)
    def _():
        ybuf[...] = jnp.zeros_like(ybuf)
        rolled(lambda r: scatter_slab(r, n_asg + r))
        scatter_wait()
        rolled(lambda r: gather_row(0, 0, r))

    @pl.when(i < n_used)
    def _():
        slot = i & 1
        gather_wait(slot)
        x = jnp.concatenate([xbuf[slot, pl.ds(s, R, stride=SLAB), :] for s in range(SLAB)], axis=1).astype(BF16)
        nxt = jnp.minimum(i + 1, n_used - 1)
        prev = jnp.where(i > 0, i - 1, last_blk)
        for r in range(R):
            gather_row(nxt, 1 - slot, r)
            scatter_row(prev, r)
        hid = _silu(_dot(x, w1_ref[0])) * _dot(x, w3_ref[0])
        y = _dot(hid.astype(BF16), w2_ref[0])
        scatter_wait()
        for s in range(SLAB):
            ybuf[pl.ds(s, R, stride=SLAB), :] = y[:, s * LANES:(s + 1) * LANES]

        @pl.when(i == n_used - 1)
        def _():
            rolled(lambda r: scatter_row(i, r))
            scatter_wait()
            gather_wait(1 - slot)


def _experts(blk_e, n_used, run0, n_live, order, hn_slabs, w1, w3, w2):
    n_asg = order.shape[0]
    nblk = blk_e.shape[0]
    wspec = lambda a: pl.BlockSpec((1,) + a.shape[1:], lambda i, be, *_: (be[i], 0, 0))
    grid_spec = pltpu.PrefetchScalarGridSpec(
        num_scalar_prefetch=5,
        grid=(nblk,),
        in_specs=[pl.BlockSpec(memory_space=pl.ANY), wspec(w1), wspec(w3), wspec(w2)],
        out_specs=pl.BlockSpec(memory_space=pl.ANY),
        scratch_shapes=[pltpu.VMEM((2, MOE_BLOCK * SLAB, LANES), F32), pltpu.VMEM((MOE_BLOCK * SLAB, LANES), F32),
                        pltpu.SemaphoreType.DMA((2,)), pltpu.SemaphoreType.DMA],
    )
    return pl.pallas_call(
        functools.partial(_expert_body, n_asg=n_asg),
        grid_spec=grid_spec,
        out_shape=jax.ShapeDtypeStruct((n_asg + MOE_BLOCK, SLAB, LANES), F32),
        compiler_params=_params(("arbitrary",)),
        name="experts",
    )(blk_e, n_used, run0, n_live, order, hn_slabs, w1, w3, w2)


def _combine_body(h_ref, y0_ref, y1_ref, rt_ref, o_ref, *, tm):
    rt = rt_ref[...]
    w0, w1 = rt[:, 2:3], rt[:, 3:4]
    for s in range(SLAB):
        cols = slice(s * LANES, (s + 1) * LANES)
        y0 = y0_ref[pl.ds(s, tm, stride=SLAB), :]
        y1 = y1_ref[pl.ds(s, tm, stride=SLAB), :]
        o_ref[:, cols] = h_ref[:, cols] + (w0 * y0 + w1 * y1)


def _combine(h, y_slabs, route, *, tm):
    n_tok = h.shape[0]
    nb = n_tok // tm
    y2d = y_slabs.reshape(-1, LANES)
    return pl.pallas_call(
        functools.partial(_combine_body, tm=tm),
        grid=(nb,),
        in_specs=[pl.BlockSpec((tm, D_MODEL), lambda i: (i, 0)),
                  pl.BlockSpec((tm * SLAB, LANES), lambda i: (i, 0)),
                  pl.BlockSpec((tm * SLAB, LANES), lambda i: (nb + i, 0)),
                  pl.BlockSpec((tm, LANES), lambda i: (i, 0))],
        out_specs=pl.BlockSpec((tm, D_MODEL), lambda i: (i, 0)),
        out_shape=jax.ShapeDtypeStruct((n_tok, D_MODEL), F32),
        compiler_params=_params(("parallel",)),
        name="moe_combine",
    )(h, y2d, y2d, route)


def _pair_groups(w):
    rows, cols = w.shape
    z = jnp.zeros((NSA_GROUPS, rows, NSA_GROUPS, cols), w.dtype)
    for g in range(NSA_GROUPS):
        z = z.at[g, :, g, :].set(w)
    return z


def _compress_weights(w1, w2, pos):
    w1r = w1.reshape(2, CMP_STRIDE, HEAD_DIM, CMP_HIDDEN)
    halves = []
    for half in range(2):
        z = jnp.zeros((CMP_STRIDE, NSA_GROUPS, HEAD_DIM, NSA_GROUPS, CMP_HIDDEN), F32)
        for g in range(NSA_GROUPS):
            z = z.at[:, g, :, g, :].set(w1r[half])
        halves.append(z.reshape(CMP_STRIDE * KV_DIM, NSA_GROUPS * CMP_HIDDEN))
    w1p = jnp.stack(halves).astype(BF16)
    w2p = _pair_groups(w2).reshape(NSA_GROUPS * CMP_HIDDEN, KV_DIM).astype(BF16)
    posr = pos.reshape(2, CMP_STRIDE, 1, HEAD_DIM)
    posp = jnp.broadcast_to(posr, (2, CMP_STRIDE, NSA_GROUPS, HEAD_DIM)).reshape(2, CMP_STRIDE * KV_DIM)
    return w1p, w2p, posp


def kernel(x, attn_norm_g, ffn_norm_g, w_in, w_out, nsa_q_norm_g, nsa_k_norm_g, cmp_pos_k, cmp_pos_v,
           cmp_wk1, cmp_wk2, cmp_wv1, cmp_wv2, nsa_out_norm_g, hgrn_lb_logits, hgrn_out_norm_g,
           moe_w_group, moe_b_group, moe_w_expert, moe_b_expert, moe_w1, moe_w3, moe_w2):
    batch, seq, _ = x.shape
    n_tok = batch * seq
    assert seq % SEL_CHUNK == 0 and seq >= WINDOW + 128 and seq // SEL_BLOCK <= HEAD_DIM
    depth = w_in.shape[0]
    lb_all = jnp.cumsum(jax.nn.softmax(hgrn_lb_logits.astype(F32), axis=0), axis=0)
    tm = min(512, n_tok)
    h = x.reshape(n_tok, D_MODEL)
    for layer in range(depth):
        wl = w_in[layer]
        gate_cols = wl[:, QKV_COLS:QKV_COLS + NSA_HEADS * NSA_BRANCHES]
        w_in_p = jnp.concatenate(
            [wl[:, :QKV_COLS], wl[:, QKV_COLS + NSA_HEADS * NSA_BRANCHES:],
             jnp.pad(gate_cols, ((0, 0), (0, LANES - NSA_HEADS * NSA_BRANCHES)))], axis=1).astype(BF16)
        gq128 = jnp.tile(nsa_q_norm_g[layer], 2)[None, :]
        gk128 = jnp.tile(nsa_k_norm_g[layer], (1, 2))
        g_attn = attn_norm_g[layer][None, :]

        qt, ksel, vselt, kwin, vwint, kc_raw, vc_raw, gates_t, hg = _inproj(
            h, g_attn, w_in_p, gq128, gk128, seq=seq, tm=tm)

        wk1p, wk2p, poskp = _compress_weights(cmp_wk1[layer], cmp_wk2[layer], cmp_pos_k[layer])
        wv1p, wv2p, posvp = _compress_weights(cmp_wv1[layer], cmp_wv2[layer], cmp_pos_v[layer])
        nrow = seq // CMP_STRIDE
        kcmp, vcmpt = _compress(kc_raw.reshape(batch, nrow, CMP_STRIDE * KV_DIM),
                                vc_raw.reshape(batch, nrow, CMP_STRIDE * KV_DIM),
                                poskp, posvp, wk1p, wv1p, wk2p, wv2p, gk128)

        o_nsa = _attention(qt, kcmp, vcmpt, ksel, vselt, kwin, vwint, gates_t,
                           batch=batch, seq=seq, tq=256)
        o_hgrn = _hgrn(hg, lb_all[layer][None, :], hgrn_out_norm_g[layer][None, :],
                       batch=batch, seq=seq, tc=min(512, seq))

        wo = w_out[layer].astype(BF16)
        wr = jnp.concatenate([moe_w_group[layer], moe_w_expert[layer]], axis=1)
        wr = jnp.pad(wr, ((0, 0), (0, LANES - wr.shape[1])))
        wr_hi = wr.astype(BF16)
        wr_lo = (wr - wr_hi.astype(F32)).astype(BF16)
        br = jnp.pad(jnp.concatenate([moe_b_group[layer], moe_b_expert[layer]]),
                     (0, LANES - MOE_GROUPS - N_EXPERTS))[None, :]
        h, hn, route = _outproj(o_nsa, o_hgrn, h, nsa_out_norm_g[layer][None, :],
                                wo[:NSA_WIDTH], wo[NSA_WIDTH:], ffn_norm_g[layer][None, :],
                                jnp.stack([wr_hi, wr_lo]), br, tm=tm)

        e_flat = route[:, 0:2].astype(I32).reshape(-1)
        n_asg = e_flat.shape[0]
        order = jnp.argsort(e_flat, stable=True).astype(I32)
        counts = jnp.sum((e_flat[:, None] == jnp.arange(N_EXPERTS)[None, :]).astype(I32), axis=0)
        start = jnp.cumsum(counts) - counts
        padded = (counts + MOE_BLOCK - 1) // MOE_BLOCK * MOE_BLOCK
        pend = jnp.cumsum(padded)
        p_rows = n_asg + N_EXPERTS * MOE_BLOCK
        nblk = p_rows // MOE_BLOCK
        blk_e = jnp.minimum(jnp.sum(jnp.arange(nblk)[:, None] * MOE_BLOCK >= pend[None, :], axis=1),
                            N_EXPERTS - 1).astype(I32)
        n_used = (pend[-1] // MOE_BLOCK).astype(I32)[None]
        r0 = jnp.arange(nblk, dtype=I32) * MOE_BLOCK - (pend - padded)[blk_e]
        n_live = jnp.clip(counts[blk_e] - r0, 0, MOE_BLOCK).astype(I32)
        run0 = jnp.clip(start[blk_e] + r0, 0, n_asg).astype(I32)
        y_slabs = _experts(blk_e, n_used, run0, n_live, order, hn.reshape(n_tok, SLAB, LANES),
                           moe_w1[layer].astype(BF16), moe_w3[layer].astype(BF16), moe_w2[layer].astype(BF16))
        h = _combine(h, y_slabs, route, tm=min(256, n_tok))
    return h.reshape(batch, seq, D_MODEL)
```
